```python
import jax, jax.numpy as jnp
from jax import lax
import numpy as np

D_MODEL = 1024
BATCH = 16
SEQ = 2048
DEPTH = 2

CHUNK = 64
RW_HEAD_DIM = 64
RW_WIDTH = D_MODEL // 2
RW_HEADS = RW_WIDTH // RW_HEAD_DIM
DECAY_LORA = 64
AAA_LORA = 64
AT_HEAD_DIM = 64
AT_WIDTH = D_MODEL // 2
AT_HEADS = AT_WIDTH // AT_HEAD_DIM
LEFT_CHUNKS = 8
BAND_CHUNKS = LEFT_CHUNKS + 1
REL_CLIP = 2 * CHUNK
SG_CHUNK = 128
SG_WIDTH = D_MODEL
SG_GROUPS = 8
SG_GROUP_DIM = SG_WIDTH // SG_GROUPS
SHIFT_WIDTH = 3 * RW_WIDTH + DECAY_LORA + AAA_LORA
EVEN_IN = SHIFT_WIDTH + RW_WIDTH + 4 * AT_WIDTH
EVEN_MIX = RW_WIDTH + AT_WIDTH
ODD_IN = 3 * SG_WIDTH
RMS_EPS = 1e-6
LN_EPS = 1e-5
GN_EPS = 64e-5
NEG_INF = -1e30

kernel_name = "hybrid_rwkv7_chunkattn_gmlp_encoder"


def rmsnorm(x, g):
    x32 = x.astype(jnp.float32)
    y = x32 * lax.rsqrt(jnp.mean(x32 * x32, axis=-1, keepdims=True) + RMS_EPS)
    return (y * g.astype(jnp.float32)).astype(x.dtype)


def token_shift(p, mu):
    p_prev = jnp.pad(p, ((0, 0), (1, 0), (0, 0)))[:, :-1]
    return p + (p_prev - p) * mu


def rwkv7_mix(p_r, p_k, p_v, p_wd, p_ad, w0, w2, a0, a2, k_k, k_a, r_k, lnx_g, lnx_b):
    B, S, W = p_r.shape
    H, N = RW_HEADS, RW_HEAD_DIM
    f32 = jnp.float32
    r = p_r.astype(f32)
    k = p_k.astype(f32)
    v = p_v.astype(f32)
    w = -jax.nn.softplus(-(w0.astype(f32) + jnp.tanh(p_wd.astype(f32)) @ w2.astype(f32))) - 0.5
    decay = jnp.exp(-jnp.exp(w))
    a = jax.nn.sigmoid(a0.astype(f32) + p_ad.astype(f32) @ a2.astype(f32))
    heads = lambda t: t.reshape(B, S, H, N)
    kk = heads(k * k_k.astype(f32))
    kk = kk / jnp.maximum(jnp.sqrt(jnp.sum(kk * kk, axis=-1, keepdims=True)), 1e-12)
    k = k * (1.0 + (a - 1.0) * k_a.astype(f32))
    r_h, k_h, v_h, a_h = heads(r), heads(k), heads(v), heads(a)
    tm = lambda t: jnp.moveaxis(t, 1, 0)

    def step(state, inp):
        r_t, w_t, k_t, v_t, a_t, b_t = inp
        sa = jnp.einsum('bhvk,bhk->bhv', state, a_t)
        state = (state * w_t[:, :, None, :]
                 + sa[..., None] * b_t[:, :, None, :]
                 + v_t[..., None] * k_t[:, :, None, :])
        y_t = jnp.einsum('bhvk,bhk->bhv', state, r_t)
        return state, y_t

    s0 = jnp.zeros((B, H, N, N), f32)
    _, y = lax.scan(step, s0, (tm(r_h), tm(heads(decay)), tm(k_h), tm(v_h), tm(-kk), tm(kk * a_h)))
    y = jnp.moveaxis(y, 0, 1)
    mu = jnp.mean(y, axis=-1, keepdims=True)
    var = jnp.mean(jnp.square(y - mu), axis=-1, keepdims=True)
    y = ((y - mu) * lax.rsqrt(var + GN_EPS)).reshape(B, S, W)
    y = y * lnx_g.astype(f32) + lnx_b.astype(f32)
    bonus = jnp.sum(r_h * k_h * r_k.astype(f32), axis=-1, keepdims=True) * v_h
    y = y + bonus.reshape(B, S, W)
    return y.astype(p_r.dtype)


def chunk_attention(q, k, v, bias_table):
    B, S, W = q.shape
    H, Dh, L = AT_HEADS, AT_HEAD_DIM, CHUNK
    NC = S // L
    to_chunks = lambda t: t.reshape(B, NC, L, H, Dh).transpose(0, 3, 1, 2, 4)
    pad = ((0, 0), (0, 0), (LEFT_CHUNKS, 0), (0, 0), (0, 0))
    qc = to_chunks(q)
    kp = jnp.pad(to_chunks(k), pad)
    vp = jnp.pad(to_chunks(v), pad)
    qi = jnp.arange(L)
    kj = jnp.arange(BAND_CHUNKS * L)
    rel = LEFT_CHUNKS * L + qi[:, None] - kj[None, :]
    idx = jnp.clip(rel, -REL_CLIP, REL_CLIP) + REL_CLIP
    bias = bias_table[:, idx].astype(jnp.float32)
    scale = 1.0 / np.sqrt(Dh)

    def one_chunk(args):
        q_blk, c = args
        kb = lax.dynamic_slice_in_dim(kp, c, BAND_CHUNKS, axis=2).reshape(B, H, BAND_CHUNKS * L, Dh)
        vb = lax.dynamic_slice_in_dim(vp, c, BAND_CHUNKS, axis=2).reshape(B, H, BAND_CHUNKS * L, Dh)
        s = jnp.einsum('bhqd,bhkd->bhqk', q_blk, kb).astype(jnp.float32) * scale + bias
        valid = kj >= (LEFT_CHUNKS - c) * L
        s = jnp.where(valid[None, None, None, :], s, NEG_INF)
        p = jax.nn.softmax(s, axis=-1)
        return jnp.einsum('bhqk,bhkd->bhqd', p.astype(vb.dtype), vb)

    out = lax.map(one_chunk, (jnp.moveaxis(qc, 2, 0), jnp.arange(NC)))
    return out.transpose(1, 0, 3, 2, 4).reshape(B, S, W)


def spatial_gating(u, v, ln_g, ln_b, sg_w, sg_b):
    B, S, W = u.shape
    NB = S // SG_CHUNK
    v32 = v.astype(jnp.float32)
    mu = jnp.mean(v32, axis=-1, keepdims=True)
    var = jnp.mean(jnp.square(v32 - mu), axis=-1, keepdims=True)
    v = ((v32 - mu) * lax.rsqrt(var + LN_EPS) * ln_g.astype(jnp.float32)
         + ln_b.astype(jnp.float32)).astype(u.dtype)
    pos = jnp.arange(SG_CHUNK)
    mask = (pos[None, :] // CHUNK) <= (pos[:, None] // CHUNK)
    w = sg_w * mask[None].astype(sg_w.dtype)
    vb = v.reshape(B, NB, SG_CHUNK, SG_GROUPS, SG_GROUP_DIM)
    sv = jnp.einsum('gij,bnjgc->bnigc', w, vb) + sg_b.T[None, None, :, :, None]
    return u * sv.reshape(B, S, W)


def even_layer(h, g, w_in, shift_mu, w0, w2, a0, a2, k_k, k_a, r_k, lnx_g, lnx_b, att_bias, w_out):
    p = rmsnorm(h, g) @ w_in
    p_shift = token_shift(p[..., :SHIFT_WIDTH], shift_mu)
    p_r, p_k, p_v, p_wd, p_ad = jnp.split(
        p_shift, [RW_WIDTH, 2 * RW_WIDTH, 3 * RW_WIDTH, 3 * RW_WIDTH + DECAY_LORA], axis=-1)
    rest = p[..., SHIFT_WIDTH:]
    gate_a, q_b, k_b, v_b, gate_b = jnp.split(
        rest, [RW_WIDTH, RW_WIDTH + AT_WIDTH, RW_WIDTH + 2 * AT_WIDTH, RW_WIDTH + 3 * AT_WIDTH], axis=-1)
    y_a = rwkv7_mix(p_r, p_k, p_v, p_wd, p_ad, w0, w2, a0, a2, k_k, k_a, r_k, lnx_g, lnx_b)
    y_a = y_a * jax.nn.silu(gate_a)
    y_b = chunk_attention(q_b, k_b, v_b, att_bias) * jax.nn.silu(gate_b)
    return h + jnp.concatenate([y_a, y_b], axis=-1) @ w_out


def odd_layer(h, g, w_in, ln_g, ln_b, sg_w, sg_b, w_out):
    p = rmsnorm(h, g) @ w_in
    u, v, gate = jnp.split(p, [SG_WIDTH, 2 * SG_WIDTH], axis=-1)
    y = spatial_gating(jax.nn.gelu(u), jax.nn.gelu(v), ln_g, ln_b, sg_w, sg_b)
    return h + (y * jax.nn.silu(gate)) @ w_out


def setup_inputs(seed: int = 0) -> dict:
    key = jax.random.key(seed)
    ks = jax.random.split(key, 24)
    ne = (DEPTH + 1) // 2
    no = DEPTH // 2
    nrm = lambda k, shape, s: jax.random.normal(k, shape, jnp.float32) * s
    return {
        "x": nrm(ks[0], (BATCH, SEQ, D_MODEL), 1.0),
        "norm_g": 1.0 + nrm(ks[1], (DEPTH, D_MODEL), 0.01),
        "w_in_e": nrm(ks[2], (ne, D_MODEL, EVEN_IN), D_MODEL ** -0.5),
        "shift_mu": jax.random.uniform(ks[3], (ne, SHIFT_WIDTH), jnp.float32),
        "rw_w0": jax.random.uniform(ks[4], (ne, RW_WIDTH), jnp.float32, -4.0, 1.0),
        "rw_w2": nrm(ks[5], (ne, DECAY_LORA, RW_WIDTH), 0.5 * DECAY_LORA ** -0.5),
        "rw_a0": nrm(ks[6], (ne, RW_WIDTH), 0.1),
        "rw_a2": nrm(ks[7], (ne, AAA_LORA, RW_WIDTH), 0.5 * AAA_LORA ** -0.5),
        "rw_kk": 0.85 + nrm(ks[8], (ne, RW_WIDTH), 0.02),
        "rw_ka": 1.0 + nrm(ks[9], (ne, RW_WIDTH), 0.02),
        "rw_rk": nrm(ks[10], (ne, RW_HEADS, RW_HEAD_DIM), 0.1),
        "rw_lnx_g": 1.0 + nrm(ks[11], (ne, RW_WIDTH), 0.01),
        "rw_lnx_b": nrm(ks[12], (ne, RW_WIDTH), 0.01),
        "att_bias": nrm(ks[13], (ne, AT_HEADS, 2 * REL_CLIP + 1), 0.1),
        "w_out_e": nrm(ks[14], (ne, EVEN_MIX, D_MODEL), 0.5 * EVEN_MIX ** -0.5),
        "w_in_o": nrm(ks[15], (no, D_MODEL, ODD_IN), D_MODEL ** -0.5),
        "sg_ln_g": 1.0 + nrm(ks[16], (no, SG_WIDTH), 0.01),
        "sg_ln_b": nrm(ks[17], (no, SG_WIDTH), 0.01),
        "sg_w": nrm(ks[18], (no, SG_GROUPS, SG_CHUNK, SG_CHUNK), SG_CHUNK ** -0.5),
        "sg_b": 1.0 + nrm(ks[19], (no, SG_GROUPS, SG_CHUNK), 0.01),
        "w_out_o": nrm(ks[20], (no, SG_WIDTH, D_MODEL), 0.5 * SG_WIDTH ** -0.5),
        "final_g": 1.0 + nrm(ks[21], (D_MODEL,), 0.01),
    }


def reference(x, norm_g, w_in_e, shift_mu, rw_w0, rw_w2, rw_a0, rw_a2, rw_kk, rw_ka, rw_rk,
              rw_lnx_g, rw_lnx_b, att_bias, w_out_e, w_in_o, sg_ln_g, sg_ln_b, sg_w, sg_b,
              w_out_o, final_g):
    h = x
    for layer in range(DEPTH):
        i = layer // 2
        if layer % 2 == 0:
            h = even_layer(h, norm_g[layer], w_in_e[i], shift_mu[i], rw_w0[i], rw_w2[i], rw_a0[i],
                           rw_a2[i], rw_kk[i], rw_ka[i], rw_rk[i], rw_lnx_g[i], rw_lnx_b[i],
                           att_bias[i], w_out_e[i])
        else:
            h = odd_layer(h, norm_g[layer], w_in_o[i], sg_ln_g[i], sg_ln_b[i], sg_w[i], sg_b[i],
                          w_out_o[i])
    return rmsnorm(h, final_g)
```

```python
import functools
import math

import jax
import jax.numpy as jnp
from jax import lax
from jax.experimental import pallas as pl
from jax.experimental.pallas import tpu as pltpu

F32 = jnp.float32
BF16 = jnp.bfloat16

D_MODEL = 1024
CHUNK = 64
RW_HEAD_DIM = 64
RW_WIDTH = 512
DECAY_LORA = 64
AAA_LORA = 64
AT_HEAD_DIM = 64
AT_WIDTH = 512
LEFT_CHUNKS = 8
BAND = (LEFT_CHUNKS + 1) * CHUNK
REL_CLIP = 2 * CHUNK
SG_CHUNK = 128
SG_WIDTH = 1024
SG_GROUPS = 8
SHIFT_WIDTH = 3 * RW_WIDTH + DECAY_LORA + AAA_LORA
EVEN_IN = SHIFT_WIDTH + RW_WIDTH + 4 * AT_WIDTH
RMS_EPS = 1e-6
LN_EPS = 1e-5
GN_EPS = 64e-5
NEG_INF = -1e30

LANES = 128
VMEM_LIMIT = 56 * 1024 * 1024

PAIR = 2 * RW_HEAD_DIM


def _dot(a, b):
    return jnp.dot(a.astype(BF16), b.astype(BF16), preferred_element_type=F32)


def _dot_nt(a, b):
    return lax.dot_general(a.astype(BF16), b.astype(BF16), (((1,), (1,)), ((), ())),
                           preferred_element_type=F32)


def _dot_tn(a, b):
    return lax.dot_general(a.astype(BF16), b.astype(BF16), (((0,), (0,)), ((), ())),
                           preferred_element_type=F32)


def _split(x):
    hi = x.astype(BF16)
    lo = (x - hi.astype(F32)).astype(BF16)
    return hi, lo


def _dot3(a, b):
    ah, al = _split(a)
    bh, bl = _split(b)
    return _dot(ah, bh) + (_dot(ah, bl) + _dot(al, bh))


def _dot_exact_lhs(a_bf16, x):
    hi = x.astype(BF16)
    r1 = x - hi.astype(F32)
    mid = r1.astype(BF16)
    lo = (r1 - mid.astype(F32)).astype(BF16)
    return _dot(a_bf16, hi) + (_dot(a_bf16, mid) + _dot(a_bf16, lo))


def _dot_exact_rhs(x, b_bf16):
    hi = x.astype(BF16)
    r1 = x - hi.astype(F32)
    mid = r1.astype(BF16)
    lo = (r1 - mid.astype(F32)).astype(BF16)
    return _dot(hi, b_bf16) + (_dot(mid, b_bf16) + _dot(lo, b_bf16))


def _sigmoid(x):
    return 1.0 / (1.0 + jnp.exp(-x))


def _silu(x):
    return x * _sigmoid(x)


def _gelu_tanh(x):
    c = math.sqrt(2.0 / math.pi)
    return 0.5 * x * (1.0 + jnp.tanh(c * (x + 0.044715 * (x * x * x))))


def _rmsnorm(x, g):
    ms = jnp.mean(x * x, axis=-1, keepdims=True)
    return x * lax.rsqrt(ms + RMS_EPS) * g


def _col_chunks(lo, hi, width=256):
    out = []
    c = lo
    while c < hi:
        w = min(width, hi - c)
        out.append((c, w))
        c += w
    return out


def _in_proj_even_kernel(x_ref, g_ref, w_ref, mu_ref, rw_ref, ga_ref, q_ref, k_ref, v_ref, gb_ref,
                         n_ref, carry_ref, *, tiles_per_seq):
    tm = x_ref.shape[0]
    i = pl.program_id(0)
    n_ref[...] = _rmsnorm(x_ref[...], g_ref[...]).astype(BF16)

    @pl.when(i % tiles_per_seq == 0)
    def _():
        carry_ref[...] = jnp.zeros_like(carry_ref)

    for c0, cw in _col_chunks(0, SHIFT_WIDTH):
        pc = jnp.dot(n_ref[...], w_ref[:, c0:c0 + cw], preferred_element_type=F32)
        row0 = lax.broadcasted_iota(jnp.int32, (tm, cw), 0) == 0
        prev = jnp.where(row0, carry_ref[0:1, c0:c0 + cw], pltpu.roll(pc, 1, 0))
        carry_ref[0:1, c0:c0 + cw] = pc[tm - 1:tm, :]
        rw_ref[:, c0:c0 + cw] = pc + (prev - pc) * mu_ref[:, c0:c0 + cw]

    base = SHIFT_WIDTH
    for dst, scale in ((ga_ref, None), (q_ref, 1.0 / math.sqrt(AT_HEAD_DIM)), (k_ref, None),
                       (v_ref, None), (gb_ref, None)):
        for c0, cw in _col_chunks(0, RW_WIDTH):
            pc = jnp.dot(n_ref[...], w_ref[:, base + c0:base + c0 + cw], preferred_element_type=F32)
            if scale is not None:
                pc = pc * scale
            dst[:, c0:c0 + cw] = pc.astype(dst.dtype)
        base += RW_WIDTH


def _in_proj_even(x2d, g, w_bf16, mu, seq_len, tm):
    t = x2d.shape[0]
    row = lambda i: (i, 0)
    const = lambda i: (0, 0)
    kern = functools.partial(_in_proj_even_kernel, tiles_per_seq=seq_len // tm)
    return pl.pallas_call(
        kern,
        grid=(t // tm,),
        in_specs=[pl.BlockSpec((tm, D_MODEL), row),
                  pl.BlockSpec((1, D_MODEL), const),
                  pl.BlockSpec((D_MODEL, EVEN_IN), const),
                  pl.BlockSpec((1, SHIFT_WIDTH), const)],
        out_specs=[pl.BlockSpec((tm, SHIFT_WIDTH), row),
                   pl.BlockSpec((tm, RW_WIDTH), row),
                   pl.BlockSpec((tm, AT_WIDTH), row),
                   pl.BlockSpec((tm, AT_WIDTH), row),
                   pl.BlockSpec((tm, AT_WIDTH), row),
                   pl.BlockSpec((tm, AT_WIDTH), row)],
        out_shape=[jax.ShapeDtypeStruct((t, SHIFT_WIDTH), F32),
                   jax.ShapeDtypeStruct((t, RW_WIDTH), F32),
                   jax.ShapeDtypeStruct((t, AT_WIDTH), BF16),
                   jax.ShapeDtypeStruct((t, AT_WIDTH), BF16),
                   jax.ShapeDtypeStruct((t, AT_WIDTH), BF16),
                   jax.ShapeDtypeStruct((t, AT_WIDTH), F32)],
        scratch_shapes=[pltpu.VMEM((tm, D_MODEL), BF16),
                        pltpu.VMEM((8, SHIFT_WIDTH), F32)],
        compiler_params=pltpu.CompilerParams(dimension_semantics=("arbitrary",),
                                             vmem_limit_bytes=VMEM_LIMIT),
        name="in_proj_even",
    )(x2d, g, w_bf16, mu)


EXP_NEG_HALF = math.exp(-0.5)


def _tri_inv(a, eye, blk):
    a8 = jnp.where(blk[8], a, 0.0)
    b1 = eye + a8
    a2 = _dot3(a8, a8)
    b2 = b1 + _dot3(a2, b1)
    a4 = _dot3(a2, a2)
    d = b2 + _dot3(a4, b2)
    for sz in (8, 16, 32):
        e = jnp.where(jnp.logical_and(blk[2 * sz], jnp.logical_not(blk[sz])), a, 0.0)
        d = d + _dot3(d, _dot3(e, d))
    return d


def _rwkv_kernel(rw_ref, ga_ref, w0_ref, w2_ref, a0_ref, a2_ref, kk_ref, ka_ref, rk_ref, lg_ref,
                 lb_ref, o_ref, z_ref):
    rb = rw_ref.shape[0]
    n_pairs = RW_WIDTH // PAIR

    @pl.when(pl.program_id(1) == 0)
    def _():
        z_ref[...] = jnp.zeros_like(z_ref)

    ri = lax.broadcasted_iota(jnp.int32, (PAIR, PAIR), 0)
    ci = lax.broadcasted_iota(jnp.int32, (PAIR, PAIR), 1)
    same_head = (ri // RW_HEAD_DIM) == (ci // RW_HEAD_DIM)
    tril_bd = jnp.logical_and(same_head, ci <= ri)
    stril_bd = jnp.logical_and(same_head, ci < ri)
    eye_mask = ri == ci
    eye = jnp.where(eye_mask, 1.0, 0.0).astype(F32)
    blk = {sz: (ri // sz) == (ci // sz) for sz in (8, 16, 32, 64)}
    ones_bd = jnp.where(same_head, 1.0, 0.0).astype(BF16)
    t64r = lax.broadcasted_iota(jnp.int32, (CHUNK, CHUNK), 0)
    t64c = lax.broadcasted_iota(jnp.int32, (CHUNK, CHUNK), 1)
    tri64 = jnp.where(t64c <= t64r, 1.0, 0.0).astype(BF16)
    lane_head = lax.broadcasted_iota(jnp.int32, (CHUNK, PAIR), 1) // RW_HEAD_DIM
    head0 = lane_head == 0
    head1 = lane_head == 1

    def head_sum(x):
        return jnp.concatenate(
            [_dot_exact_rhs(x[:, p * PAIR:(p + 1) * PAIR], ones_bd) for p in range(n_pairs)], axis=1)

    def bd(x, p):
        xs = x[:, p * PAIR:(p + 1) * PAIR]
        return jnp.concatenate([jnp.where(head0, xs, 0.0), jnp.where(head1, xs, 0.0)], axis=0)

    w0 = w0_ref[...]
    a0 = a0_ref[...]
    k_k = kk_ref[...]
    k_a = ka_ref[...]
    r_k = rk_ref[...]

    for cc in range(rb // CHUNK):
        rows = slice(cc * CHUNK, (cc + 1) * CHUNK)
        r = rw_ref[rows, 0:RW_WIDTH]
        k = rw_ref[rows, RW_WIDTH:2 * RW_WIDTH]
        v = rw_ref[rows, 2 * RW_WIDTH:3 * RW_WIDTH]
        wd = rw_ref[rows, 3 * RW_WIDTH:3 * RW_WIDTH + DECAY_LORA]
        ad = rw_ref[rows, 3 * RW_WIDTH + DECAY_LORA:SHIFT_WIDTH]

        zw = w0 + _dot3(jnp.tanh(wd), w2_ref[...])
        lw = -EXP_NEG_HALF * _sigmoid(zw)
        c = _dot_exact_lhs(tri64, lw)
        c_last = c[CHUNK - 1:CHUNK, :]
        e_c = jnp.exp(c)
        e_nc = jnp.exp(-c)
        e_cm = jnp.exp(c - lw)
        e_lc = jnp.exp(c_last - c)
        g_last = jnp.exp(c_last)

        a_g = _sigmoid(a0 + _dot3(ad, a2_ref[...]))
        kk = k * k_k
        ss = head_sum(kk * kk)
        kkn = kk * lax.rsqrt(jnp.maximum(ss, 1e-24))
        k_m = k * (1.0 + (a_g - 1.0) * k_a)
        b_v = kkn * a_g
        rh = r * e_c
        kh = k_m * e_nc
        bh = b_v * e_nc
        ah = -kkn * e_cm
        bt = b_v * e_lc
        kt = k_m * e_lc
        bonus = head_sum(r * k_m * r_k) * v

        y_pairs = []
        for p in range(n_pairs):
            sl = slice(p * PAIR, (p + 1) * PAIR)
            rh_bd = bd(rh, p)
            ah_bd = bd(ah, p)
            v_bd = bd(v, p)
            lhs = jnp.concatenate([rh_bd, ah_bd], axis=0)
            rhs = jnp.concatenate([kh[:, sl], kh[:, sl], bh[:, sl], bh[:, sl]], axis=0)
            g = _dot_nt(lhs, rhs)
            a_rk = jnp.where(tril_bd, g[0:PAIR, 0:PAIR], 0.0)
            a_rb = jnp.where(tril_bd, g[0:PAIR, PAIR:2 * PAIR], 0.0)
            a_ak = jnp.where(stril_bd, g[PAIR:2 * PAIR, 0:PAIR], 0.0)
            a_ab = jnp.where(stril_bd, g[PAIR:2 * PAIR, PAIR:2 * PAIR], 0.0)
            t_inv = _tri_inv(a_ab, eye, blk)
            x1 = _dot3(a_ak, v_bd)
            wu = _dot3(t_inv, jnp.concatenate([ah_bd, x1], axis=1))
            qy = _dot3(a_rb, wu)
            q = rh_bd + qy[:, 0:PAIR]
            y0 = qy[:, PAIR:2 * PAIR] + _dot3(a_rk, v_bd)
            mn = _dot_tn3(bd(bt, p), wu)
            m = mn[:, 0:PAIR] + jnp.where(eye_mask, g_last[:, sl], 0.0)
            n = mn[:, PAIR:2 * PAIR] + _dot_tn3(bd(kt, p), v_bd)
            z = z_ref[p]
            zy = _dot3(jnp.concatenate([m, q], axis=0), z)
            z_ref[p] = zy[0:PAIR] + n
            y_bd = zy[PAIR:2 * PAIR] + y0
            y_pairs.append(y_bd[0:CHUNK] + y_bd[CHUNK:PAIR])
        y = jnp.concatenate(y_pairs, axis=1)

        mu = head_sum(y) * (1.0 / RW_HEAD_DIM)
        d = y - mu
        var = head_sum(d * d) * (1.0 / RW_HEAD_DIM)
        yn = d * lax.rsqrt(var + GN_EPS) * lg_ref[...] + lb_ref[...] + bonus
        o_ref[rows, :] = (yn * _silu(ga_ref[rows, :])).astype(o_ref.dtype)


def _dot_tn3(a, b):
    ah, al = _split(a)
    bh, bl = _split(b)
    return _dot_tn(ah, bh) + (_dot_tn(ah, bl) + _dot_tn(al, bh))


def _rwkv(rw, ga, w0, w2, a0, a2, k_k, k_a, r_k, lnx_g, lnx_b, rb):
    b, s, _ = rw.shape
    blk3 = lambda i, j: (i, j, 0)
    const = lambda i, j: (0, 0)
    vec = pl.BlockSpec((1, RW_WIDTH), const)
    lora = pl.BlockSpec((DECAY_LORA, RW_WIDTH), const)
    return pl.pallas_call(
        _rwkv_kernel,
        grid=(b, s // rb),
        in_specs=[pl.BlockSpec((None, rb, SHIFT_WIDTH), blk3),
                  pl.BlockSpec((None, rb, RW_WIDTH), blk3),
                  vec, lora, vec, lora, vec, vec, vec, vec, vec],
        out_specs=pl.BlockSpec((None, rb, RW_WIDTH), blk3),
        out_shape=jax.ShapeDtypeStruct((b, s, RW_WIDTH), BF16),
        scratch_shapes=[pltpu.VMEM((RW_WIDTH // PAIR, PAIR, PAIR), F32)],
        compiler_params=pltpu.CompilerParams(dimension_semantics=("parallel", "arbitrary"),
                                             vmem_limit_bytes=VMEM_LIMIT),
        name="rwkv7_mix",
    )(rw, ga, w0, w2, a0, a2, k_k, k_a, r_k, lnx_g, lnx_b)


SLAB = 4 * AT_HEAD_DIM


def _attn_kernel(q_ref, k_ref, v_ref, gb_ref, bias_ref, o_ref):
    cb = q_ref.shape[0] // CHUNK
    j = pl.program_id(1)
    lane_head = lax.broadcasted_iota(jnp.int32, (CHUNK, SLAB), 1) // AT_HEAD_DIM
    head_masks = [lane_head == h for h in range(SLAB // AT_HEAD_DIM)]
    key_row = lax.broadcasted_iota(jnp.int32, (BAND, SLAB), 0)
    for cc in range(cb):
        c = j * cb + cc
        start = pl.multiple_of(c * CHUNK, CHUNK)
        valid = key_row >= (LEFT_CHUNKS - c) * CHUNK
        rows = slice(cc * CHUNK, (cc + 1) * CHUNK)
        for s0 in range(0, AT_WIDTH, SLAB):
            cols = slice(s0, s0 + SLAB)
            qs = q_ref[rows, cols]
            q_bd = jnp.concatenate([jnp.where(mh, qs, jnp.zeros_like(qs)) for mh in head_masks], axis=0)
            kb = k_ref[pl.ds(start, BAND), cols]
            st = lax.dot_general(kb, q_bd, (((1,), (1,)), ((), ())), preferred_element_type=F32)
            st = jnp.where(valid, st + bias_ref[:, cols], NEG_INF)
            mx = jnp.max(st, axis=0, keepdims=True)
            e = jnp.exp(st - mx)
            l = jnp.sum(e, axis=0, keepdims=True)
            pt = (e * (1.0 / l)).astype(BF16)
            vb = v_ref[pl.ds(start, BAND), cols]
            o_bd = lax.dot_general(pt, vb, (((0,), (0,)), ((), ())), preferred_element_type=F32)
            o = jnp.zeros((CHUNK, SLAB), F32)
            for h, mh in enumerate(head_masks):
                o = jnp.where(mh, o_bd[h * CHUNK:(h + 1) * CHUNK, :], o)
            o_ref[rows, cols] = (o * _silu(gb_ref[rows, cols])).astype(o_ref.dtype)


def _attn(q, k_pad, v_pad, gb, bias_t, cb):
    b, s, _ = q.shape
    sp = k_pad.shape[1]
    qblk = lambda i, j: (i, j, 0)
    full = lambda i, j: (i, 0, 0)
    return pl.pallas_call(
        _attn_kernel,
        grid=(b, s // (cb * CHUNK)),
        in_specs=[pl.BlockSpec((None, cb * CHUNK, AT_WIDTH), qblk),
                  pl.BlockSpec((None, sp, AT_WIDTH), full),
                  pl.BlockSpec((None, sp, AT_WIDTH), full),
                  pl.BlockSpec((None, cb * CHUNK, AT_WIDTH), qblk),
                  pl.BlockSpec((BAND, AT_WIDTH), lambda i, j: (0, 0))],
        out_specs=pl.BlockSpec((None, cb * CHUNK, AT_WIDTH), qblk),
        out_shape=jax.ShapeDtypeStruct((b, s, AT_WIDTH), BF16),
        compiler_params=pltpu.CompilerParams(dimension_semantics=("parallel", "arbitrary"),
                                             vmem_limit_bytes=VMEM_LIMIT),
        name="chunk_attention",
    )(q, k_pad, v_pad, gb, bias_t)


def _layer1_kernel(x_ref, ya_ref, yb_ref, woe_ref, g1_ref, wio_ref, lng_ref, lnb_ref, sgw_ref,
                   sgb_ref, woo_ref, fg_ref, o_ref):
    tm = x_ref.shape[0]
    gd = SG_WIDTH // SG_GROUPS
    h1 = (x_ref[...] + jnp.dot(ya_ref[...], woe_ref[0:RW_WIDTH, :], preferred_element_type=F32)
          + jnp.dot(yb_ref[...], woe_ref[RW_WIDTH:, :], preferred_element_type=F32))
    n1 = _rmsnorm(h1, g1_ref[...]).astype(BF16)
    u = _gelu_tanh(jnp.dot(n1, wio_ref[:, 0:SG_WIDTH], preferred_element_type=F32))
    vv = _gelu_tanh(jnp.dot(n1, wio_ref[:, SG_WIDTH:2 * SG_WIDTH], preferred_element_type=F32))
    gate = jnp.dot(n1, wio_ref[:, 2 * SG_WIDTH:], preferred_element_type=F32)
    mu = jnp.mean(vv, axis=-1, keepdims=True)
    dv = vv - mu
    var = jnp.mean(dv * dv, axis=-1, keepdims=True)
    vln = (dv * lax.rsqrt(var + LN_EPS) * lng_ref[...] + lnb_ref[...]).astype(BF16)

    pr = lax.broadcasted_iota(jnp.int32, (SG_CHUNK, SG_CHUNK), 0) // CHUNK
    pc = lax.broadcasted_iota(jnp.int32, (SG_CHUNK, SG_CHUNK), 1) // CHUNK
    causal = pc <= pr
    sv_cols = []
    for g in range(SG_GROUPS):
        wg = jnp.where(causal, sgw_ref[g], 0.0).astype(BF16)
        cols = slice(g * gd, (g + 1) * gd)
        blocks = [jnp.dot(wg, vln[nb * SG_CHUNK:(nb + 1) * SG_CHUNK, cols], preferred_element_type=F32)
                  + sgb_ref[:, cols] for nb in range(tm // SG_CHUNK)]
        sv_cols.append(jnp.concatenate(blocks, axis=0))
    sv = jnp.concatenate(sv_cols, axis=1)
    y = ((u * sv) * _silu(gate)).astype(BF16)
    h2 = h1 + jnp.dot(y, woo_ref[...], preferred_element_type=F32)
    o_ref[...] = _rmsnorm(h2, fg_ref[...])


def _layer1(x2d, ya, yb, woe, g1, wio, lng, lnb, sgw, sgb_full, woo, fg, tm):
    t = x2d.shape[0]
    row = lambda i: (i, 0)
    const = lambda i: (0, 0)
    vec = pl.BlockSpec((1, D_MODEL), const)
    return pl.pallas_call(
        _layer1_kernel,
        grid=(t // tm,),
        in_specs=[pl.BlockSpec((tm, D_MODEL), row),
                  pl.BlockSpec((tm, RW_WIDTH), row),
                  pl.BlockSpec((tm, AT_WIDTH), row),
                  pl.BlockSpec((D_MODEL, D_MODEL), const),
                  vec,
                  pl.BlockSpec((D_MODEL, 3 * SG_WIDTH), const),
                  vec, vec,
                  pl.BlockSpec((SG_GROUPS, SG_CHUNK, SG_CHUNK), lambda i: (0, 0, 0)),
                  pl.BlockSpec((SG_CHUNK, SG_WIDTH), const),
                  pl.BlockSpec((SG_WIDTH, D_MODEL), const),
                  vec],
        out_specs=pl.BlockSpec((tm, D_MODEL), row),
        out_shape=jax.ShapeDtypeStruct((t, D_MODEL), F32),
        compiler_params=pltpu.CompilerParams(dimension_semantics=("parallel",),
                                             vmem_limit_bytes=VMEM_LIMIT),
        name="gmlp_layer",
    )(x2d, ya, yb, woe, g1, wio, lng, lnb, sgw, sgb_full, woo, fg)


def _bias_table_t(att_bias):
    qi = jnp.arange(CHUNK)
    kj = jnp.arange(BAND)
    rel = LEFT_CHUNKS * CHUNK + qi[None, :] - kj[:, None]
    idx = jnp.clip(rel, -REL_CLIP, REL_CLIP) + REL_CLIP
    bt = att_bias[:, idx]
    return jnp.transpose(bt, (1, 0, 2)).reshape(BAND, -1).astype(F32)


def kernel(x, norm_g, w_in_e, shift_mu, rw_w0, rw_w2, rw_a0, rw_a2, rw_kk, rw_ka, rw_rk, rw_lnx_g,
           rw_lnx_b, att_bias, w_out_e, w_in_o, sg_ln_g, sg_ln_b, sg_w, sg_b, w_out_o, final_g):
    b, s, d = x.shape
    assert d == D_MODEL and s % 512 == 0
    x2d = x.reshape(b * s, d)
    row = lambda a: a.reshape(1, -1).astype(F32)

    rw, ga, q, k, v, gb = _in_proj_even(x2d, row(norm_g[0]), w_in_e[0].astype(BF16), row(shift_mu[0]),
                                        seq_len=s, tm=512)
    r3 = lambda a: a.reshape(b, s, a.shape[-1])
    ya = _rwkv(r3(rw), r3(ga), row(rw_w0[0]), rw_w2[0], row(rw_a0[0]), rw_a2[0], row(rw_kk[0]),
               row(rw_ka[0]), row(rw_rk[0]), row(rw_lnx_g[0]), row(rw_lnx_b[0]), rb=128)
    pad = ((0, 0), (LEFT_CHUNKS * CHUNK, 0), (0, 0))
    yb = _attn(r3(q), jnp.pad(r3(k), pad), jnp.pad(r3(v), pad), r3(gb), _bias_table_t(att_bias[0]), cb=4)

    sgb_full = jnp.repeat(sg_b[0].T, SG_WIDTH // SG_GROUPS, axis=1).astype(F32)
    out = _layer1(x2d, ya.reshape(b * s, -1), yb.reshape(b * s, -1), w_out_e[0].astype(BF16),
                  row(norm_g[1]), w_in_o[0].astype(BF16), row(sg_ln_g[0]), row(sg_ln_b[0]), sg_w[0],
                  sgb_full, w_out_o[0].astype(BF16), row(final_g), tm=256)
    return out.reshape(b, s, d)
```

```python
import functools
import math

import jax
import jax.numpy as jnp
import numpy as np
from jax import lax
from jax.experimental import pallas as pl
from jax.experimental.pallas import tpu as pltpu

F32 = jnp.float32
BF16 = jnp.bfloat16

D_MODEL = 1024
CHUNK = 64
RW_HEAD_DIM = 64
RW_WIDTH = 512
DECAY_LORA = 64
AAA_LORA = 64
AT_HEAD_DIM = 64
AT_WIDTH = 512
LEFT_CHUNKS = 8
BAND = (LEFT_CHUNKS + 1) * CHUNK
REL_CLIP = 2 * CHUNK
SG_CHUNK = 128
SG_WIDTH = 1024
SG_GROUPS = 8
SHIFT_WIDTH = 3 * RW_WIDTH + DECAY_LORA + AAA_LORA
EVEN_IN = SHIFT_WIDTH + RW_WIDTH + 4 * AT_WIDTH
RMS_EPS = 1e-6
LN_EPS = 1e-5
GN_EPS = 64e-5
NEG_INF = -1e30

LANES = 128
VMEM_LIMIT = 56 * 1024 * 1024

PAIR = 2 * RW_HEAD_DIM


def _dot(a, b):
    return jnp.dot(a.astype(BF16), b.astype(BF16), preferred_element_type=F32)


def _dot_nt(a, b):
    return lax.dot_general(a.astype(BF16), b.astype(BF16), (((1,), (1,)), ((), ())),
                           preferred_element_type=F32)


def _dot_tn(a, b):
    return lax.dot_general(a.astype(BF16), b.astype(BF16), (((0,), (0,)), ((), ())),
                           preferred_element_type=F32)


def _split(x):
    hi = x.astype(BF16)
    lo = (x - hi.astype(F32)).astype(BF16)
    return hi, lo


def _dot_exact_lhs(a_bf16, x):
    hi, lo = _split(x)
    return _dot(a_bf16, hi) + _dot(a_bf16, lo)


def _dot_exact_rhs(x, b_bf16):
    hi, lo = _split(x)
    return _dot(hi, b_bf16) + _dot(lo, b_bf16)


def _sigmoid(x):
    return 1.0 / (1.0 + jnp.exp(-x))


def _silu(x):
    return x * _sigmoid(x)


def _gelu_tanh(x):
    c = math.sqrt(2.0 / math.pi)
    return 0.5 * x * (1.0 + jnp.tanh(c * (x + 0.044715 * (x * x * x))))


def _rmsnorm(x, g):
    ms = jnp.mean(x * x, axis=-1, keepdims=True)
    return x * lax.rsqrt(ms + RMS_EPS) * g


def _col_chunks(lo, hi, width=256):
    out = []
    c = lo
    while c < hi:
        w = min(width, hi - c)
        out.append((c, w))
        c += w
    return out


def _in_proj_even_kernel(x_ref, g_ref, w_ref, mu_ref, rw_ref, ga_ref, q_ref, k_ref, v_ref, gb_ref,
                         n_ref, carry_ref, *, tiles_per_seq):
    tm = x_ref.shape[0]
    i = pl.program_id(0)
    n_ref[...] = _rmsnorm(x_ref[...], g_ref[...]).astype(BF16)

    @pl.when(i % tiles_per_seq == 0)
    def _():
        carry_ref[...] = jnp.zeros_like(carry_ref)

    for c0, cw in _col_chunks(0, SHIFT_WIDTH):
        pc = jnp.dot(n_ref[...], w_ref[:, c0:c0 + cw], preferred_element_type=F32)
        row0 = lax.broadcasted_iota(jnp.int32, (tm, cw), 0) == 0
        prev = jnp.where(row0, carry_ref[0:1, c0:c0 + cw], pltpu.roll(pc, 1, 0))
        carry_ref[0:1, c0:c0 + cw] = pc[tm - 1:tm, :]
        rw_ref[:, c0:c0 + cw] = pc + (prev - pc) * mu_ref[:, c0:c0 + cw]

    base = SHIFT_WIDTH
    for dst, scale in ((ga_ref, None), (q_ref, 1.0 / math.sqrt(AT_HEAD_DIM)), (k_ref, None),
                       (v_ref, None), (gb_ref, None)):
        for c0, cw in _col_chunks(0, RW_WIDTH):
            pc = jnp.dot(n_ref[...], w_ref[:, base + c0:base + c0 + cw], preferred_element_type=F32)
            if scale is not None:
                pc = pc * scale
            dst[:, c0:c0 + cw] = pc.astype(dst.dtype)
        base += RW_WIDTH


def _in_proj_even(x2d, g, w_bf16, mu, seq_len, tm):
    t = x2d.shape[0]
    row = lambda i: (i, 0)
    const = lambda i: (0, 0)
    kern = functools.partial(_in_proj_even_kernel, tiles_per_seq=seq_len // tm)
    return pl.pallas_call(
        kern,
        grid=(t // tm,),
        in_specs=[pl.BlockSpec((tm, D_MODEL), row),
                  pl.BlockSpec((1, D_MODEL), const),
                  pl.BlockSpec((D_MODEL, EVEN_IN), const),
                  pl.BlockSpec((1, SHIFT_WIDTH), const)],
        out_specs=[pl.BlockSpec((tm, SHIFT_WIDTH), row),
                   pl.BlockSpec((tm, RW_WIDTH), row),
                   pl.BlockSpec((tm, AT_WIDTH), row),
                   pl.BlockSpec((tm, AT_WIDTH), row),
                   pl.BlockSpec((tm, AT_WIDTH), row),
                   pl.BlockSpec((tm, AT_WIDTH), row)],
        out_shape=[jax.ShapeDtypeStruct((t, SHIFT_WIDTH), F32),
                   jax.ShapeDtypeStruct((t, RW_WIDTH), F32),
                   jax.ShapeDtypeStruct((t, AT_WIDTH), BF16),
                   jax.ShapeDtypeStruct((t, AT_WIDTH), BF16),
                   jax.ShapeDtypeStruct((t, AT_WIDTH), BF16),
                   jax.ShapeDtypeStruct((t, AT_WIDTH), F32)],
        scratch_shapes=[pltpu.VMEM((tm, D_MODEL), BF16),
                        pltpu.VMEM((8, SHIFT_WIDTH), F32)],
        compiler_params=pltpu.CompilerParams(dimension_semantics=("arbitrary",),
                                             vmem_limit_bytes=VMEM_LIMIT),
        name="in_proj_even",
    )(x2d, g, w_bf16, mu)


EXP_NEG_HALF = math.exp(-0.5)


def _rwkv_kernel(rw_ref, ga_ref, w0_ref, w2_ref, a0_ref, a2_ref, kk_ref, ka_ref, rk_ref, lg_ref,
                 lb_ref, o_ref, z_ref):
    rb = rw_ref.shape[0]
    n_chunks = rb // CHUNK
    n_pairs = RW_WIDTH // PAIR

    @pl.when(pl.program_id(1) == 0)
    def _():
        z_ref[...] = jnp.zeros_like(z_ref)

    ri = lax.broadcasted_iota(jnp.int32, (PAIR, PAIR), 0)
    ci = lax.broadcasted_iota(jnp.int32, (PAIR, PAIR), 1)
    same_head = (ri // RW_HEAD_DIM) == (ci // RW_HEAD_DIM)
    tril_bd = jnp.logical_and(same_head, ci <= ri)
    stril_bd = jnp.logical_and(same_head, ci < ri)
    eye_mask = ri == ci
    eye = jnp.where(eye_mask, 1.0, 0.0).astype(F32)
    blk = {sz: (ri // sz) == (ci // sz) for sz in (8, 16, 32, 64)}
    merge = {sz: jnp.logical_and(blk[2 * sz], jnp.logical_not(blk[sz])) for sz in (8, 16, 32)}
    ones_bd = jnp.where(same_head, 1.0, 0.0).astype(BF16)
    t64r = lax.broadcasted_iota(jnp.int32, (CHUNK, CHUNK), 0)
    t64c = lax.broadcasted_iota(jnp.int32, (CHUNK, CHUNK), 1)
    tri64 = jnp.where(t64c <= t64r, 1.0, 0.0).astype(BF16)
    lane_head = lax.broadcasted_iota(jnp.int32, (CHUNK, PAIR), 1) // RW_HEAD_DIM
    head0 = lane_head == 0
    head1 = lane_head == 1

    def head_sum(x):
        return jnp.concatenate(
            [_dot_exact_rhs(x[:, p * PAIR:(p + 1) * PAIR], ones_bd) for p in range(n_pairs)], axis=1)

    def bd(x, p):
        xs = x[:, p * PAIR:(p + 1) * PAIR]
        return jnp.concatenate([jnp.where(head0, xs, 0.0), jnp.where(head1, xs, 0.0)], axis=0)

    w0 = w0_ref[...]
    a0 = a0_ref[...]
    k_k = kk_ref[...]
    k_a = ka_ref[...]
    r_k = rk_ref[...]

    chains = [(cc, p) for cc in range(n_chunks) for p in range(n_pairs)]
    bonus = {}
    g_mat, rh_bd, ah_bd, v_bd, bt_t, kt_t, g_last = {}, {}, {}, {}, {}, {}, {}
    for cc in range(n_chunks):
        rows = slice(cc * CHUNK, (cc + 1) * CHUNK)
        r = rw_ref[rows, 0:RW_WIDTH]
        k = rw_ref[rows, RW_WIDTH:2 * RW_WIDTH]
        v = rw_ref[rows, 2 * RW_WIDTH:3 * RW_WIDTH]
        wd = rw_ref[rows, 3 * RW_WIDTH:3 * RW_WIDTH + DECAY_LORA]
        ad = rw_ref[rows, 3 * RW_WIDTH + DECAY_LORA:SHIFT_WIDTH]

        zw = w0 + _dot(jnp.tanh(wd), w2_ref[...])
        lw = -EXP_NEG_HALF * _sigmoid(zw)
        c = _dot_exact_lhs(tri64, lw)
        c_last = c[CHUNK - 1:CHUNK, :]
        e_c = jnp.exp(c)
        e_nc = jnp.exp(-c)
        e_cm = jnp.exp(c - lw)
        e_lc = jnp.exp(c_last - c)
        g_l = jnp.exp(c_last)

        a_g = _sigmoid(a0 + _dot(ad, a2_ref[...]))
        kk = k * k_k
        ss = head_sum(kk * kk)
        kkn = kk * lax.rsqrt(jnp.maximum(ss, 1e-24))
        k_m = k * (1.0 + (a_g - 1.0) * k_a)
        b_v = kkn * a_g
        rh = r * e_c
        kh = k_m * e_nc
        bh = b_v * e_nc
        ah = -kkn * e_cm
        bt = b_v * e_lc
        kt = k_m * e_lc
        bonus[cc] = head_sum(r * k_m * r_k) * v
        for p in range(n_pairs):
            sl = slice(p * PAIR, (p + 1) * PAIR)
            ch = (cc, p)
            rh_bd[ch] = bd(rh, p)
            ah_bd[ch] = bd(ah, p)
            v_bd[ch] = bd(v, p)
            bt_t[ch] = bd(bt, p).T
            kt_t[ch] = bd(kt, p).T
            g_last[ch] = g_l[:, sl]
            lhs = jnp.concatenate([rh_bd[ch], ah_bd[ch]], axis=0)
            rhs = jnp.concatenate([kh[:, sl], kh[:, sl], bh[:, sl], bh[:, sl]], axis=0)
            g_mat[ch] = _dot_nt(lhs, rhs)

    a_rk = {ch: jnp.where(tril_bd, g_mat[ch][0:PAIR, 0:PAIR], 0.0) for ch in chains}
    a_rb = {ch: jnp.where(tril_bd, g_mat[ch][0:PAIR, PAIR:2 * PAIR], 0.0) for ch in chains}
    a_ak = {ch: jnp.where(stril_bd, g_mat[ch][PAIR:2 * PAIR, 0:PAIR], 0.0) for ch in chains}
    a_ab = {ch: jnp.where(stril_bd, g_mat[ch][PAIR:2 * PAIR, PAIR:2 * PAIR], 0.0) for ch in chains}

    a8 = {ch: jnp.where(blk[8], a_ab[ch], 0.0) for ch in chains}
    a2 = {ch: _dot(a8[ch], a8[ch]) for ch in chains}
    b1 = {ch: eye + a8[ch] for ch in chains}
    b2 = {ch: b1[ch] + _dot(a2[ch], b1[ch]) for ch in chains}
    a4 = {ch: _dot(a2[ch], a2[ch]) for ch in chains}
    t_inv = {ch: b2[ch] + _dot(a4[ch], b2[ch]) for ch in chains}
    for sz in (8, 16, 32):
        ed = {ch: _dot(jnp.where(merge[sz], a_ab[ch], 0.0), t_inv[ch]) for ch in chains}
        t_inv = {ch: t_inv[ch] + _dot(t_inv[ch], ed[ch]) for ch in chains}

    x1 = {ch: _dot(a_ak[ch], v_bd[ch]) for ch in chains}
    wu = {ch: _dot(t_inv[ch], jnp.concatenate([ah_bd[ch], x1[ch]], axis=1)) for ch in chains}
    zeros = jnp.zeros((PAIR, PAIR), F32)
    qymn = {}
    for ch in chains:
        lhs = jnp.concatenate([jnp.concatenate([a_rb[ch], a_rk[ch]], axis=1),
                               jnp.concatenate([bt_t[ch], kt_t[ch]], axis=1)], axis=0)
        rhs = jnp.concatenate([wu[ch], jnp.concatenate([zeros, v_bd[ch]], axis=1)], axis=0)
        qymn[ch] = _dot(lhs, rhs)

    z = [z_ref[p] for p in range(n_pairs)]
    for cc in range(n_chunks):
        rows = slice(cc * CHUNK, (cc + 1) * CHUNK)
        y_pairs = []
        for p in range(n_pairs):
            ch = (cc, p)
            q = rh_bd[ch] + qymn[ch][0:PAIR, 0:PAIR]
            m = qymn[ch][PAIR:2 * PAIR, 0:PAIR] + jnp.where(eye_mask, g_last[ch], 0.0)
            zy = _dot(jnp.concatenate([m, q], axis=0), z[p])
            z[p] = zy[0:PAIR] + qymn[ch][PAIR:2 * PAIR, PAIR:2 * PAIR]
            y_bd = zy[PAIR:2 * PAIR] + qymn[ch][0:PAIR, PAIR:2 * PAIR]
            y_pairs.append(y_bd[0:CHUNK] + y_bd[CHUNK:PAIR])
        y = jnp.concatenate(y_pairs, axis=1)
        mu = head_sum(y) * (1.0 / RW_HEAD_DIM)
        d = y - mu
        var = head_sum(d * d) * (1.0 / RW_HEAD_DIM)
        yn = d * lax.rsqrt(var + GN_EPS) * lg_ref[...] + lb_ref[...] + bonus[cc]
        o_ref[rows, :] = (yn * _silu(ga_ref[rows, :])).astype(o_ref.dtype)
    for p in range(n_pairs):
        z_ref[p] = z[p]


def _rwkv(rw, ga, w0, w2, a0, a2, k_k, k_a, r_k, lnx_g, lnx_b, rb):
    b, s, _ = rw.shape
    blk3 = lambda i, j: (i, j, 0)
    const = lambda i, j: (0, 0)
    vec = pl.BlockSpec((1, RW_WIDTH), const)
    lora = pl.BlockSpec((DECAY_LORA, RW_WIDTH), const)
    return pl.pallas_call(
        _rwkv_kernel,
        grid=(b, s // rb),
        in_specs=[pl.BlockSpec((None, rb, SHIFT_WIDTH), blk3),
                  pl.BlockSpec((None, rb, RW_WIDTH), blk3),
                  vec, lora, vec, lora, vec, vec, vec, vec, vec],
        out_specs=pl.BlockSpec((None, rb, RW_WIDTH), blk3),
        out_shape=jax.ShapeDtypeStruct((b, s, RW_WIDTH), BF16),
        scratch_shapes=[pltpu.VMEM((RW_WIDTH // PAIR, PAIR, PAIR), F32)],
        compiler_params=pltpu.CompilerParams(dimension_semantics=("parallel", "arbitrary"),
                                             vmem_limit_bytes=VMEM_LIMIT),
        name="rwkv7_mix",
    )(rw, ga, w0, w2, a0, a2, k_k, k_a, r_k, lnx_g, lnx_b)


SLAB = 4 * AT_HEAD_DIM


def _attn_kernel(q_ref, k_ref, v_ref, gb_ref, bias_ref, o_ref):
    cb = q_ref.shape[0] // CHUNK
    j = pl.program_id(1)
    lane_head = lax.broadcasted_iota(jnp.int32, (CHUNK, SLAB), 1) // AT_HEAD_DIM
    head_masks = [lane_head == h for h in range(SLAB // AT_HEAD_DIM)]
    key_row = lax.broadcasted_iota(jnp.int32, (BAND, SLAB), 0)
    for cc in range(cb):
        c = j * cb + cc
        start = pl.multiple_of(c * CHUNK, CHUNK)
        valid = key_row >= (LEFT_CHUNKS - c) * CHUNK
        rows = slice(cc * CHUNK, (cc + 1) * CHUNK)
        for s0 in range(0, AT_WIDTH, SLAB):
            cols = slice(s0, s0 + SLAB)
            qs = q_ref[rows, cols]
            q_bd = jnp.concatenate([jnp.where(mh, qs, jnp.zeros_like(qs)) for mh in head_masks], axis=0)
            kb = k_ref[pl.ds(start, BAND), cols]
            st = lax.dot_general(kb, q_bd, (((1,), (1,)), ((), ())), preferred_element_type=F32)
            st = jnp.where(valid, st + bias_ref[:, cols], NEG_INF)
            mx = jnp.max(st, axis=0, keepdims=True)
            e = jnp.exp(st - mx)
            l = jnp.sum(e, axis=0, keepdims=True)
            pt = (e * (1.0 / l)).astype(BF16)
            vb = v_ref[pl.ds(start, BAND), cols]
            o_bd = lax.dot_general(pt, vb, (((0,), (0,)), ((), ())), preferred_element_type=F32)
            o = jnp.zeros((CHUNK, SLAB), F32)
            for h, mh in enumerate(head_masks):
                o = jnp.where(mh, o_bd[h * CHUNK:(h + 1) * CHUNK, :], o)
            o_ref[rows, cols] = (o * _silu(gb_ref[rows, cols])).astype(o_ref.dtype)


def _attn(q, k_pad, v_pad, gb, bias_t, cb):
    b, s, _ = q.shape
    sp = k_pad.shape[1]
    qblk = lambda i, j: (i, j, 0)
    full = lambda i, j: (i, 0, 0)
    return pl.pallas_call(
        _attn_kernel,
        grid=(b, s // (cb * CHUNK)),
        in_specs=[pl.BlockSpec((None, cb * CHUNK, AT_WIDTH), qblk),
                  pl.BlockSpec((None, sp, AT_WIDTH), full),
                  pl.BlockSpec((None, sp, AT_WIDTH), full),
                  pl.BlockSpec((None, cb * CHUNK, AT_WIDTH), qblk),
                  pl.BlockSpec((BAND, AT_WIDTH), lambda i, j: (0, 0))],
        out_specs=pl.BlockSpec((None, cb * CHUNK, AT_WIDTH), qblk),
        out_shape=jax.ShapeDtypeStruct((b, s, AT_WIDTH), BF16),
        compiler_params=pltpu.CompilerParams(dimension_semantics=("parallel", "arbitrary"),
                                             vmem_limit_bytes=VMEM_LIMIT),
        name="chunk_attention",
    )(q, k_pad, v_pad, gb, bias_t)


def _layer1_kernel(x_ref, ya_ref, yb_ref, woe_ref, g1_ref, wio_ref, lng_ref, lnb_ref, sgw_ref,
                   sgb_ref, woo_ref, fg_ref, o_ref):
    tm = x_ref.shape[0]
    gd = SG_WIDTH // SG_GROUPS
    h1 = (x_ref[...] + jnp.dot(ya_ref[...], woe_ref[0:RW_WIDTH, :], preferred_element_type=F32)
          + jnp.dot(yb_ref[...], woe_ref[RW_WIDTH:, :], preferred_element_type=F32))
    n1 = _rmsnorm(h1, g1_ref[...]).astype(BF16)
    u = _gelu_tanh(jnp.dot(n1, wio_ref[:, 0:SG_WIDTH], preferred_element_type=F32))
    vv = _gelu_tanh(jnp.dot(n1, wio_ref[:, SG_WIDTH:2 * SG_WIDTH], preferred_element_type=F32))
    gate = jnp.dot(n1, wio_ref[:, 2 * SG_WIDTH:], preferred_element_type=F32)
    mu = jnp.mean(vv, axis=-1, keepdims=True)
    dv = vv - mu
    var = jnp.mean(dv * dv, axis=-1, keepdims=True)
    vln = (dv * lax.rsqrt(var + LN_EPS) * lng_ref[...] + lnb_ref[...]).astype(BF16)

    pr = lax.broadcasted_iota(jnp.int32, (SG_CHUNK, SG_CHUNK), 0) // CHUNK
    pc = lax.broadcasted_iota(jnp.int32, (SG_CHUNK, SG_CHUNK), 1) // CHUNK
    causal = pc <= pr
    sv_cols = []
    for g in range(SG_GROUPS):
        wg = jnp.where(causal, sgw_ref[g], 0.0).astype(BF16)
        cols = slice(g * gd, (g + 1) * gd)
        blocks = [jnp.dot(wg, vln[nb * SG_CHUNK:(nb + 1) * SG_CHUNK, cols], preferred_element_type=F32)
                  + sgb_ref[:, cols] for nb in range(tm // SG_CHUNK)]
        sv_cols.append(jnp.concatenate(blocks, axis=0))
    sv = jnp.concatenate(sv_cols, axis=1)
    y = ((u * sv) * _silu(gate)).astype(BF16)
    h2 = h1 + jnp.dot(y, woo_ref[...], preferred_element_type=F32)
    o_ref[...] = _rmsnorm(h2, fg_ref[...])


def _layer1(x2d, ya, yb, woe, g1, wio, lng, lnb, sgw, sgb_full, woo, fg, tm):
    t = x2d.shape[0]
    row = lambda i: (i, 0)
    const = lambda i: (0, 0)
    vec = pl.BlockSpec((1, D_MODEL), const)
    return pl.pallas_call(
        _layer1_kernel,
        grid=(t // tm,),
        in_specs=[pl.BlockSpec((tm, D_MODEL), row),
                  pl.BlockSpec((tm, RW_WIDTH), row),
                  pl.BlockSpec((tm, AT_WIDTH), row),
                  pl.BlockSpec((D_MODEL, D_MODEL), const),
                  vec,
                  pl.BlockSpec((D_MODEL, 3 * SG_WIDTH), const),
                  vec, vec,
                  pl.BlockSpec((SG_GROUPS, SG_CHUNK, SG_CHUNK), lambda i: (0, 0, 0)),
                  pl.BlockSpec((SG_CHUNK, SG_WIDTH), const),
                  pl.BlockSpec((SG_WIDTH, D_MODEL), const),
                  vec],
        out_specs=pl.BlockSpec((tm, D_MODEL), row),
        out_shape=jax.ShapeDtypeStruct((t, D_MODEL), F32),
        compiler_params=pltpu.CompilerParams(dimension_semantics=("parallel",),
                                             vmem_limit_bytes=VMEM_LIMIT),
        name="gmlp_layer",
    )(x2d, ya, yb, woe, g1, wio, lng, lnb, sgw, sgb_full, woo, fg)


def _bias_table_t(att_bias):
    n = np.arange(-(CHUNK - 1), BAND)
    idx = np.clip(LEFT_CHUNKS * CHUNK - n, -REL_CLIP, REL_CLIP) + REL_CLIP
    diag = att_bias[:, idx]
    bt = jnp.stack([diag[:, CHUNK - 1 - qi:CHUNK - 1 - qi + BAND] for qi in range(CHUNK)], axis=-1)
    return jnp.transpose(bt, (1, 0, 2)).reshape(BAND, -1).astype(F32)


def kernel(x, norm_g, w_in_e, shift_mu, rw_w0, rw_w2, rw_a0, rw_a2, rw_kk, rw_ka, rw_rk, rw_lnx_g,
           rw_lnx_b, att_bias, w_out_e, w_in_o, sg_ln_g, sg_ln_b, sg_w, sg_b, w_out_o, final_g):
    b, s, d = x.shape
    assert d == D_MODEL and s % 512 == 0
    x2d = x.reshape(b * s, d)
    row = lambda a: a.reshape(1, -1).astype(F32)

    rw, ga, q, k, v, gb = _in_proj_even(x2d, row(norm_g[0]), w_in_e[0].astype(BF16), row(shift_mu[0]),
                                        seq_len=s, tm=512)
    r3 = lambda a: a.reshape(b, s, a.shape[-1])
    ya = _rwkv(r3(rw), r3(ga), row(rw_w0[0]), rw_w2[0], row(rw_a0[0]), rw_a2[0], row(rw_kk[0]),
               row(rw_ka[0]), row(rw_rk[0]), row(rw_lnx_g[0]), row(rw_lnx_b[0]), rb=128)
    pad = ((0, 0), (LEFT_CHUNKS * CHUNK, 0), (0, 0))
    yb = _attn(r3(q), jnp.pad(r3(k), pad), jnp.pad(r3(v), pad), r3(gb), _bias_table_t(att_bias[0]), cb=4)

    sgb_full = jnp.repeat(sg_b[0].T, SG_WIDTH // SG_GROUPS, axis=1).astype(F32)
    out = _layer1(x2d, ya.reshape(b * s, -1), yb.reshape(b * s, -1), w_out_e[0].astype(BF16),
                  row(norm_g[1]), w_in_o[0].astype(BF16), row(sg_ln_g[0]), row(sg_ln_b[0]), sg_w[0],
                  sgb_full, w_out_o[0].astype(BF16), row(final_g), tm=256)
    return out.reshape(b, s, d)
```

```python
import functools
import math

import jax
import jax.numpy as jnp
import numpy as np
from jax import lax
from jax.experimental import pallas as pl
from jax.experimental.pallas import tpu as pltpu

F32 = jnp.float32
BF16 = jnp.bfloat16

D_MODEL = 1024
CHUNK = 64
RW_HEAD_DIM = 64
RW_WIDTH = 512
DECAY_LORA = 64
AAA_LORA = 64
AT_HEAD_DIM = 64
AT_WIDTH = 512
LEFT_CHUNKS = 8
BAND = (LEFT_CHUNKS + 1) * CHUNK
REL_CLIP = 2 * CHUNK
SG_CHUNK = 128
SG_WIDTH = 1024
SG_GROUPS = 8
SHIFT_WIDTH = 3 * RW_WIDTH + DECAY_LORA + AAA_LORA
EVEN_IN = SHIFT_WIDTH + RW_WIDTH + 4 * AT_WIDTH
RMS_EPS = 1e-6
LN_EPS = 1e-5
GN_EPS = 64e-5
NEG_INF = -1e30
LOG2E = math.log2(math.e)

LANES = 128
VMEM_LIMIT = 56 * 1024 * 1024

PAIR = 2 * RW_HEAD_DIM


def _dot(a, b):
    return jnp.dot(a.astype(BF16), b.astype(BF16), preferred_element_type=F32)


def _dot_nt(a, b):
    return lax.dot_general(a.astype(BF16), b.astype(BF16), (((1,), (1,)), ((), ())),
                           preferred_element_type=F32)


def _dot_tn(a, b):
    return lax.dot_general(a.astype(BF16), b.astype(BF16), (((0,), (0,)), ((), ())),
                           preferred_element_type=F32)


def _split(x):
    hi = x.astype(BF16)
    lo = (x - hi.astype(F32)).astype(BF16)
    return hi, lo


def _dot_exact_lhs(a_bf16, x):
    hi, lo = _split(x)
    return _dot(a_bf16, hi) + _dot(a_bf16, lo)


def _dot_exact_rhs(x, b_bf16):
    hi, lo = _split(x)
    return _dot(hi, b_bf16) + _dot(lo, b_bf16)


def _sigmoid(x):
    return 1.0 / (1.0 + jnp.exp(-x))


def _silu(x):
    return x * _sigmoid(x)


def _gelu_tanh(x):
    c = math.sqrt(2.0 / math.pi)
    return 0.5 * x * (1.0 + jnp.tanh(c * (x + 0.044715 * (x * x * x))))


def _rmsnorm(x, g):
    ms = jnp.mean(x * x, axis=-1, keepdims=True)
    return x * lax.rsqrt(ms + RMS_EPS) * g


def _col_chunks(lo, hi, width=256):
    out = []
    c = lo
    while c < hi:
        w = min(width, hi - c)
        out.append((c, w))
        c += w
    return out


def _in_proj_even_kernel(x_ref, g_ref, w_ref, mu_ref, rw_ref, ga_ref, q_ref, k_ref, v_ref, gb_ref,
                         n_ref, carry_ref, *, tiles_per_seq):
    tm = x_ref.shape[0]
    i = pl.program_id(0)
    n_ref[...] = _rmsnorm(x_ref[...], g_ref[...]).astype(BF16)

    @pl.when(i % tiles_per_seq == 0)
    def _():
        carry_ref[...] = jnp.zeros_like(carry_ref)

    for c0, cw in _col_chunks(0, SHIFT_WIDTH):
        pc = jnp.dot(n_ref[...], w_ref[:, c0:c0 + cw], preferred_element_type=F32)
        row0 = lax.broadcasted_iota(jnp.int32, (tm, cw), 0) == 0
        prev = jnp.where(row0, carry_ref[0:1, c0:c0 + cw], pltpu.roll(pc, 1, 0))
        carry_ref[0:1, c0:c0 + cw] = pc[tm - 1:tm, :]
        rw_ref[:, c0:c0 + cw] = pc + (prev - pc) * mu_ref[:, c0:c0 + cw]

    base = SHIFT_WIDTH
    for dst, scale in ((ga_ref, None), (q_ref, LOG2E / math.sqrt(AT_HEAD_DIM)), (k_ref, None),
                       (v_ref, None), (gb_ref, None)):
        for c0, cw in _col_chunks(0, RW_WIDTH):
            pc = jnp.dot(n_ref[...], w_ref[:, base + c0:base + c0 + cw], preferred_element_type=F32)
            if scale is not None:
                pc = pc * scale
            dst[:, c0:c0 + cw] = pc.astype(dst.dtype)
        base += RW_WIDTH


def _in_proj_even(x2d, g, w_bf16, mu, seq_len, tm):
    t = x2d.shape[0]
    row = lambda i: (i, 0)
    const = lambda i: (0, 0)
    kern = functools.partial(_in_proj_even_kernel, tiles_per_seq=seq_len // tm)
    return pl.pallas_call(
        kern,
        grid=(t // tm,),
        in_specs=[pl.BlockSpec((tm, D_MODEL), row),
                  pl.BlockSpec((1, D_MODEL), const),
                  pl.BlockSpec((D_MODEL, EVEN_IN), const),
                  pl.BlockSpec((1, SHIFT_WIDTH), const)],
        out_specs=[pl.BlockSpec((tm, SHIFT_WIDTH), row),
                   pl.BlockSpec((tm, RW_WIDTH), row),
                   pl.BlockSpec((tm, AT_WIDTH), row),
                   pl.BlockSpec((tm, AT_WIDTH), row),
                   pl.BlockSpec((tm, AT_WIDTH), row),
                   pl.BlockSpec((tm, AT_WIDTH), row)],
        out_shape=[jax.ShapeDtypeStruct((t, SHIFT_WIDTH), F32),
                   jax.ShapeDtypeStruct((t, RW_WIDTH), F32),
                   jax.ShapeDtypeStruct((t, AT_WIDTH), BF16),
                   jax.ShapeDtypeStruct((t, AT_WIDTH), BF16),
                   jax.ShapeDtypeStruct((t, AT_WIDTH), BF16),
                   jax.ShapeDtypeStruct((t, AT_WIDTH), F32)],
        scratch_shapes=[pltpu.VMEM((tm, D_MODEL), BF16),
                        pltpu.VMEM((8, SHIFT_WIDTH), F32)],
        compiler_params=pltpu.CompilerParams(dimension_semantics=("arbitrary",),
                                             vmem_limit_bytes=VMEM_LIMIT),
        name="in_proj_even",
    )(x2d, g, w_bf16, mu)


EXP_NEG_HALF = math.exp(-0.5)


def _rwkv_kernel(rw_ref, ga_ref, w0_ref, w2_ref, a0_ref, a2_ref, kk_ref, ka_ref, rk_ref, lg_ref,
                 lb_ref, o_ref, z_ref):
    rb = rw_ref.shape[0]
    n_chunks = rb // CHUNK
    n_pairs = RW_WIDTH // PAIR

    @pl.when(pl.program_id(1) == 0)
    def _():
        z_ref[...] = jnp.zeros_like(z_ref)

    ri = lax.broadcasted_iota(jnp.int32, (PAIR, PAIR), 0)
    ci = lax.broadcasted_iota(jnp.int32, (PAIR, PAIR), 1)
    same_head = (ri // RW_HEAD_DIM) == (ci // RW_HEAD_DIM)
    tril_bd = jnp.logical_and(same_head, ci <= ri)
    stril_bd = jnp.logical_and(same_head, ci < ri)
    eye_mask = ri == ci
    eye = jnp.where(eye_mask, 1.0, 0.0).astype(F32)
    blk = {sz: (ri // sz) == (ci // sz) for sz in (8, 16, 32, 64)}
    merge = {sz: jnp.logical_and(blk[2 * sz], jnp.logical_not(blk[sz])) for sz in (8, 16, 32)}
    ones_bd = jnp.where(same_head, 1.0, 0.0).astype(BF16)
    t64r = lax.broadcasted_iota(jnp.int32, (CHUNK, CHUNK), 0)
    t64c = lax.broadcasted_iota(jnp.int32, (CHUNK, CHUNK), 1)
    tri64 = jnp.where(t64c <= t64r, 1.0, 0.0).astype(BF16)
    lane_head = lax.broadcasted_iota(jnp.int32, (CHUNK, PAIR), 1) // RW_HEAD_DIM
    head0 = lane_head == 0
    head1 = lane_head == 1

    def head_sum(x):
        return jnp.concatenate(
            [_dot_exact_rhs(x[:, p * PAIR:(p + 1) * PAIR], ones_bd) for p in range(n_pairs)], axis=1)

    def bd(x, p):
        xs = x[:, p * PAIR:(p + 1) * PAIR]
        return jnp.concatenate([jnp.where(head0, xs, 0.0), jnp.where(head1, xs, 0.0)], axis=0)

    w0 = w0_ref[...]
    a0 = a0_ref[...]
    k_k = kk_ref[...]
    k_a = ka_ref[...]
    r_k = rk_ref[...]

    chains = [(cc, p) for cc in range(n_chunks) for p in range(n_pairs)]
    bonus = {}
    g_mat, rh_bd, ah_bd, v_bd, bt_t, kt_t, g_last = {}, {}, {}, {}, {}, {}, {}
    for cc in range(n_chunks):
        rows = slice(cc * CHUNK, (cc + 1) * CHUNK)
        r = rw_ref[rows, 0:RW_WIDTH]
        k = rw_ref[rows, RW_WIDTH:2 * RW_WIDTH]
        v = rw_ref[rows, 2 * RW_WIDTH:3 * RW_WIDTH]
        wd = rw_ref[rows, 3 * RW_WIDTH:3 * RW_WIDTH + DECAY_LORA]
        ad = rw_ref[rows, 3 * RW_WIDTH + DECAY_LORA:SHIFT_WIDTH]

        zw = w0 + _dot(jnp.tanh(wd), w2_ref[...])
        lw = -EXP_NEG_HALF * _sigmoid(zw)
        c = _dot_exact_lhs(tri64, lw)
        c_last = c[CHUNK - 1:CHUNK, :]
        e_c = jnp.exp(c)
        e_nc = jnp.exp(-c)
        e_cm = jnp.exp(c - lw)
        e_lc = jnp.exp(c_last - c)
        g_l = jnp.exp(c_last)

        a_g = _sigmoid(a0 + _dot(ad, a2_ref[...]))
        kk = k * k_k
        ss = head_sum(kk * kk)
        kkn = kk * lax.rsqrt(jnp.maximum(ss, 1e-24))
        k_m = k * (1.0 + (a_g - 1.0) * k_a)
        b_v = kkn * a_g
        rh = r * e_c
        kh = k_m * e_nc
        bh = b_v * e_nc
        ah = -kkn * e_cm
        bt = b_v * e_lc
        kt = k_m * e_lc
        bonus[cc] = head_sum(r * k_m * r_k) * v
        for p in range(n_pairs):
            sl = slice(p * PAIR, (p + 1) * PAIR)
            ch = (cc, p)
            rh_bd[ch] = bd(rh, p)
            ah_bd[ch] = bd(ah, p)
            v_bd[ch] = bd(v, p)
            bt_t[ch] = bd(bt, p).T
            kt_t[ch] = bd(kt, p).T
            g_last[ch] = g_l[:, sl]
            lhs = jnp.concatenate([rh_bd[ch], ah_bd[ch]], axis=0)
            rhs = jnp.concatenate([kh[:, sl], kh[:, sl], bh[:, sl], bh[:, sl]], axis=0)
            g_mat[ch] = _dot_nt(lhs, rhs)

    a_rk = {ch: jnp.where(tril_bd, g_mat[ch][0:PAIR, 0:PAIR], 0.0) for ch in chains}
    a_rb = {ch: jnp.where(tril_bd, g_mat[ch][0:PAIR, PAIR:2 * PAIR], 0.0) for ch in chains}
    a_ak = {ch: jnp.where(stril_bd, g_mat[ch][PAIR:2 * PAIR, 0:PAIR], 0.0) for ch in chains}
    a_ab = {ch: jnp.where(stril_bd, g_mat[ch][PAIR:2 * PAIR, PAIR:2 * PAIR], 0.0) for ch in chains}

    a8 = {ch: jnp.where(blk[8], a_ab[ch], 0.0) for ch in chains}
    a2 = {ch: _dot(a8[ch], a8[ch]) for ch in chains}
    b1 = {ch: eye + a8[ch] for ch in chains}
    b2 = {ch: b1[ch] + _dot(a2[ch], b1[ch]) for ch in chains}
    a4 = {ch: _dot(a2[ch], a2[ch]) for ch in chains}
    t_inv = {ch: b2[ch] + _dot(a4[ch], b2[ch]) for ch in chains}
    for sz in (8, 16, 32):
        ed = {ch: _dot(jnp.where(merge[sz], a_ab[ch], 0.0), t_inv[ch]) for ch in chains}
        t_inv = {ch: t_inv[ch] + _dot(t_inv[ch], ed[ch]) for ch in chains}

    x1 = {ch: _dot(a_ak[ch], v_bd[ch]) for ch in chains}
    wu = {ch: _dot(t_inv[ch], jnp.concatenate([ah_bd[ch], x1[ch]], axis=1)) for ch in chains}
    zeros = jnp.zeros((PAIR, PAIR), F32)
    qymn = {}
    for ch in chains:
        lhs = jnp.concatenate([jnp.concatenate([a_rb[ch], a_rk[ch]], axis=1),
                               jnp.concatenate([bt_t[ch], kt_t[ch]], axis=1)], axis=0)
        rhs = jnp.concatenate([wu[ch], jnp.concatenate([zeros, v_bd[ch]], axis=1)], axis=0)
        qymn[ch] = _dot(lhs, rhs)

    z = [z_ref[p] for p in range(n_pairs)]
    for cc in range(n_chunks):
        rows = slice(cc * CHUNK, (cc + 1) * CHUNK)
        y_pairs = []
        for p in range(n_pairs):
            ch = (cc, p)
            q = rh_bd[ch] + qymn[ch][0:PAIR, 0:PAIR]
            m = qymn[ch][PAIR:2 * PAIR, 0:PAIR] + jnp.where(eye_mask, g_last[ch], 0.0)
            zy = _dot(jnp.concatenate([m, q], axis=0), z[p])
            z[p] = zy[0:PAIR] + qymn[ch][PAIR:2 * PAIR, PAIR:2 * PAIR]
            y_bd = zy[PAIR:2 * PAIR] + qymn[ch][0:PAIR, PAIR:2 * PAIR]
            y_pairs.append(y_bd[0:CHUNK] + y_bd[CHUNK:PAIR])
        y = jnp.concatenate(y_pairs, axis=1)
        mu = head_sum(y) * (1.0 / RW_HEAD_DIM)
        d = y - mu
        var = head_sum(d * d) * (1.0 / RW_HEAD_DIM)
        yn = d * lax.rsqrt(var + GN_EPS) * lg_ref[...] + lb_ref[...] + bonus[cc]
        o_ref[rows, :] = (yn * _silu(ga_ref[rows, :])).astype(o_ref.dtype)
    for p in range(n_pairs):
        z_ref[p] = z[p]


def _rwkv(rw, ga, w0, w2, a0, a2, k_k, k_a, r_k, lnx_g, lnx_b, rb):
    b, s, _ = rw.shape
    blk3 = lambda i, j: (i, j, 0)
    const = lambda i, j: (0, 0)
    vec = pl.BlockSpec((1, RW_WIDTH), const)
    lora = pl.BlockSpec((DECAY_LORA, RW_WIDTH), const)
    return pl.pallas_call(
        _rwkv_kernel,
        grid=(b, s // rb),
        in_specs=[pl.BlockSpec((None, rb, SHIFT_WIDTH), blk3),
                  pl.BlockSpec((None, rb, RW_WIDTH), blk3),
                  vec, lora, vec, lora, vec, vec, vec, vec, vec],
        out_specs=pl.BlockSpec((None, rb, RW_WIDTH), blk3),
        out_shape=jax.ShapeDtypeStruct((b, s, RW_WIDTH), BF16),
        scratch_shapes=[pltpu.VMEM((RW_WIDTH // PAIR, PAIR, PAIR), F32)],
        compiler_params=pltpu.CompilerParams(dimension_semantics=("parallel", "arbitrary"),
                                             vmem_limit_bytes=VMEM_LIMIT),
        name="rwkv7_mix",
    )(rw, ga, w0, w2, a0, a2, k_k, k_a, r_k, lnx_g, lnx_b)


SLAB = 4 * AT_HEAD_DIM


def _attn_kernel(q_ref, k_ref, v_ref, gb_ref, bias_ref, o_ref):
    cb = q_ref.shape[0] // CHUNK
    j = pl.program_id(1)
    lane_head = lax.broadcasted_iota(jnp.int32, (CHUNK, SLAB), 1) // AT_HEAD_DIM
    head_masks = [lane_head == h for h in range(SLAB // AT_HEAD_DIM)]
    units = [(cc, s0) for cc in range(cb) for s0 in range(0, AT_WIDTH, SLAB)]

    def band(cc):
        c = j * cb + cc
        start = pl.multiple_of(jnp.maximum(c - LEFT_CHUNKS, 0) * CHUNK, CHUNK)
        boff = pl.multiple_of(jnp.maximum(LEFT_CHUNKS - c, 0) * CHUNK, CHUNK)
        return start, boff

    def scores(cc, s0):
        start, _ = band(cc)
        cols = slice(s0, s0 + SLAB)
        qs = q_ref[cc * CHUNK:(cc + 1) * CHUNK, cols]
        q_bd = jnp.concatenate([jnp.where(mh, qs, jnp.zeros_like(qs)) for mh in head_masks], axis=0)
        kb = k_ref[pl.ds(start, BAND), cols]
        return lax.dot_general(kb, q_bd, (((1,), (1,)), ((), ())), preferred_element_type=F32)

    def finish(cc, s0, st):
        start, boff = band(cc)
        rows = slice(cc * CHUNK, (cc + 1) * CHUNK)
        cols = slice(s0, s0 + SLAB)
        st = st + bias_ref[pl.ds(boff, BAND), cols]
        mx = jnp.max(st, axis=0, keepdims=True)
        e = jnp.exp2(st - mx)
        l = jnp.sum(e, axis=0, keepdims=True)
        pt = (e * (1.0 / l)).astype(BF16)
        vb = v_ref[pl.ds(start, BAND), cols]
        o_bd = lax.dot_general(pt, vb, (((0,), (0,)), ((), ())), preferred_element_type=F32)
        o = jnp.zeros((CHUNK, SLAB), F32)
        for h, mh in enumerate(head_masks):
            o = jnp.where(mh, o_bd[h * CHUNK:(h + 1) * CHUNK, :], o)
        o_ref[rows, cols] = (o * _silu(gb_ref[rows, cols])).astype(o_ref.dtype)

    st_next = scores(*units[0])
    for i, unit in enumerate(units):
        st_cur = st_next
        if i + 1 < len(units):
            st_next = scores(*units[i + 1])
        finish(*unit, st_cur)


def _attn(q, k, v, gb, bias_ext, cb):
    b, s, _ = q.shape
    qblk = lambda i, j: (i, j, 0)
    full = lambda i, j: (i, 0, 0)
    return pl.pallas_call(
        _attn_kernel,
        grid=(b, s // (cb * CHUNK)),
        in_specs=[pl.BlockSpec((None, cb * CHUNK, AT_WIDTH), qblk),
                  pl.BlockSpec((None, s, AT_WIDTH), full),
                  pl.BlockSpec((None, s, AT_WIDTH), full),
                  pl.BlockSpec((None, cb * CHUNK, AT_WIDTH), qblk),
                  pl.BlockSpec(bias_ext.shape, lambda i, j: (0, 0))],
        out_specs=pl.BlockSpec((None, cb * CHUNK, AT_WIDTH), qblk),
        out_shape=jax.ShapeDtypeStruct((b, s, AT_WIDTH), BF16),
        compiler_params=pltpu.CompilerParams(dimension_semantics=("parallel", "arbitrary"),
                                             vmem_limit_bytes=VMEM_LIMIT),
        name="chunk_attention",
    )(q, k, v, gb, bias_ext)


def _layer1_kernel(x_ref, ya_ref, yb_ref, woe_ref, g1_ref, wio_ref, lng_ref, lnb_ref, sgw_ref,
                   sgb_ref, woo_ref, fg_ref, o_ref):
    tm = x_ref.shape[0]
    gd = SG_WIDTH // SG_GROUPS
    h1 = (x_ref[...] + jnp.dot(ya_ref[...], woe_ref[0:RW_WIDTH, :], preferred_element_type=F32)
          + jnp.dot(yb_ref[...], woe_ref[RW_WIDTH:, :], preferred_element_type=F32))
    n1 = _rmsnorm(h1, g1_ref[...]).astype(BF16)
    u = _gelu_tanh(jnp.dot(n1, wio_ref[:, 0:SG_WIDTH], preferred_element_type=F32))
    vv = _gelu_tanh(jnp.dot(n1, wio_ref[:, SG_WIDTH:2 * SG_WIDTH], preferred_element_type=F32))
    gate = jnp.dot(n1, wio_ref[:, 2 * SG_WIDTH:], preferred_element_type=F32)
    mu = jnp.mean(vv, axis=-1, keepdims=True)
    dv = vv - mu
    var = jnp.mean(dv * dv, axis=-1, keepdims=True)
    vln = (dv * lax.rsqrt(var + LN_EPS) * lng_ref[...] + lnb_ref[...]).astype(BF16)

    pr = lax.broadcasted_iota(jnp.int32, (SG_CHUNK, SG_CHUNK), 0) // CHUNK
    pc = lax.broadcasted_iota(jnp.int32, (SG_CHUNK, SG_CHUNK), 1) // CHUNK
    causal = pc <= pr
    sv_cols = []
    for g in range(SG_GROUPS):
        wg = jnp.where(causal, sgw_ref[g], 0.0).astype(BF16)
        cols = slice(g * gd, (g + 1) * gd)
        blocks = [jnp.dot(wg, vln[nb * SG_CHUNK:(nb + 1) * SG_CHUNK, cols], preferred_element_type=F32)
                  + sgb_ref[:, cols] for nb in range(tm // SG_CHUNK)]
        sv_cols.append(jnp.concatenate(blocks, axis=0))
    sv = jnp.concatenate(sv_cols, axis=1)
    y = ((u * sv) * _silu(gate)).astype(BF16)
    h2 = h1 + jnp.dot(y, woo_ref[...], preferred_element_type=F32)
    o_ref[...] = _rmsnorm(h2, fg_ref[...])


def _layer1(x2d, ya, yb, woe, g1, wio, lng, lnb, sgw, sgb_full, woo, fg, tm):
    t = x2d.shape[0]
    row = lambda i: (i, 0)
    const = lambda i: (0, 0)
    vec = pl.BlockSpec((1, D_MODEL), const)
    return pl.pallas_call(
        _layer1_kernel,
        grid=(t // tm,),
        in_specs=[pl.BlockSpec((tm, D_MODEL), row),
                  pl.BlockSpec((tm, RW_WIDTH), row),
                  pl.BlockSpec((tm, AT_WIDTH), row),
                  pl.BlockSpec((D_MODEL, D_MODEL), const),
                  vec,
                  pl.BlockSpec((D_MODEL, 3 * SG_WIDTH), const),
                  vec, vec,
                  pl.BlockSpec((SG_GROUPS, SG_CHUNK, SG_CHUNK), lambda i: (0, 0, 0)),
                  pl.BlockSpec((SG_CHUNK, SG_WIDTH), const),
                  pl.BlockSpec((SG_WIDTH, D_MODEL), const),
                  vec],
        out_specs=pl.BlockSpec((tm, D_MODEL), row),
        out_shape=jax.ShapeDtypeStruct((t, D_MODEL), F32),
        compiler_params=pltpu.CompilerParams(dimension_semantics=("parallel",),
                                             vmem_limit_bytes=VMEM_LIMIT),
        name="gmlp_layer",
    )(x2d, ya, yb, woe, g1, wio, lng, lnb, sgw, sgb_full, woo, fg)


def _bias_table_t(att_bias):
    n = np.arange(-(CHUNK - 1), BAND)
    idx = np.clip(LEFT_CHUNKS * CHUNK - n, -REL_CLIP, REL_CLIP) + REL_CLIP
    diag = att_bias[:, idx]
    bt = jnp.stack([diag[:, CHUNK - 1 - qi:CHUNK - 1 - qi + BAND] for qi in range(CHUNK)], axis=-1)
    bt = jnp.transpose(bt, (1, 0, 2)).reshape(BAND, -1).astype(F32) * LOG2E
    return jnp.concatenate([bt, jnp.full((LEFT_CHUNKS * CHUNK, bt.shape[1]), NEG_INF, F32)], axis=0)


def kernel(x, norm_g, w_in_e, shift_mu, rw_w0, rw_w2, rw_a0, rw_a2, rw_kk, rw_ka, rw_rk, rw_lnx_g,
           rw_lnx_b, att_bias, w_out_e, w_in_o, sg_ln_g, sg_ln_b, sg_w, sg_b, w_out_o, final_g):
    b, s, d = x.shape
    assert d == D_MODEL and s % 512 == 0 and s >= BAND
    x2d = x.reshape(b * s, d)
    row = lambda a: a.reshape(1, -1).astype(F32)

    rw, ga, q, k, v, gb = _in_proj_even(x2d, row(norm_g[0]), w_in_e[0].astype(BF16), row(shift_mu[0]),
                                        seq_len=s, tm=512)
    r3 = lambda a: a.reshape(b, s, a.shape[-1])
    ya = _rwkv(r3(rw), r3(ga), row(rw_w0[0]), rw_w2[0], row(rw_a0[0]), rw_a2[0], row(rw_kk[0]),
               row(rw_ka[0]), row(rw_rk[0]), row(rw_lnx_g[0]), row(rw_lnx_b[0]), rb=256)
    yb = _attn(r3(q), r3(k), r3(v), r3(gb), _bias_table_t(att_bias[0]), cb=4)

    sgb_full = jnp.repeat(sg_b[0].T, SG_WIDTH // SG_GROUPS, axis=1).astype(F32)
    out = _layer1(x2d, ya.reshape(b * s, -1), yb.reshape(b * s, -1), w_out_e[0].astype(BF16),
                  row(norm_g[1]), w_in_o[0].astype(BF16), row(sg_ln_g[0]), row(sg_ln_b[0]), sg_w[0],
                  sgb_full, w_out_o[0].astype(BF16), row(final_g), tm=256)
    return out.reshape(b, s, d)
```

```python
import functools
import math

import jax
import jax.numpy as jnp
import numpy as np
from jax import lax
from jax.experimental import pallas as pl
from jax.experimental.pallas import tpu as pltpu

F32 = jnp.float32
BF16 = jnp.bfloat16

D_MODEL = 1024
CHUNK = 64
RW_HEAD_DIM = 64
RW_WIDTH = 512
DECAY_LORA = 64
AAA_LORA = 64
AT_HEAD_DIM = 64
AT_WIDTH = 512
LEFT_CHUNKS = 8
BAND = (LEFT_CHUNKS + 1) * CHUNK
REL_CLIP = 2 * CHUNK
SG_CHUNK = 128
SG_WIDTH = 1024
SG_GROUPS = 8
SHIFT_WIDTH = 3 * RW_WIDTH + DECAY_LORA + AAA_LORA
EVEN_IN = SHIFT_WIDTH + RW_WIDTH + 4 * AT_WIDTH
RMS_EPS = 1e-6
LN_EPS = 1e-5
GN_EPS = 64e-5
NEG_INF = -1e30
LOG2E = math.log2(math.e)

LANES = 128
VMEM_LIMIT = 56 * 1024 * 1024

PAIR = 2 * RW_HEAD_DIM


def _dot(a, b):
    return jnp.dot(a.astype(BF16), b.astype(BF16), preferred_element_type=F32)


def _dot_nt(a, b):
    return lax.dot_general(a.astype(BF16), b.astype(BF16), (((1,), (1,)), ((), ())),
                           preferred_element_type=F32)


def _dot_tn(a, b):
    return lax.dot_general(a.astype(BF16), b.astype(BF16), (((0,), (0,)), ((), ())),
                           preferred_element_type=F32)


def _split(x):
    hi = x.astype(BF16)
    lo = (x - hi.astype(F32)).astype(BF16)
    return hi, lo


def _dot_exact_lhs(a_bf16, x):
    hi, lo = _split(x)
    return _dot(a_bf16, hi) + _dot(a_bf16, lo)


def _dot_exact_rhs(x, b_bf16):
    hi, lo = _split(x)
    return _dot(hi, b_bf16) + _dot(lo, b_bf16)


def _sigmoid(x):
    return 1.0 / (1.0 + jnp.exp(-x))


def _silu(x):
    return x * _sigmoid(x)


def _gelu_tanh(x):
    c = math.sqrt(2.0 / math.pi)
    return 0.5 * x * (1.0 + jnp.tanh(c * (x + 0.044715 * (x * x * x))))


def _rmsnorm(x, g):
    ms = jnp.mean(x * x, axis=-1, keepdims=True)
    return x * lax.rsqrt(ms + RMS_EPS) * g


def _col_chunks(lo, hi, width=256):
    out = []
    c = lo
    while c < hi:
        w = min(width, hi - c)
        out.append((c, w))
        c += w
    return out


def _in_proj_even_kernel(x_ref, g_ref, w_ref, mu_ref, rw_ref, ga_ref, q_ref, k_ref, v_ref, gb_ref,
                         n_ref, carry_ref, *, tiles_per_seq):
    tm = x_ref.shape[0]
    i = pl.program_id(0)
    n_ref[...] = _rmsnorm(x_ref[...], g_ref[...]).astype(BF16)

    @pl.when(i % tiles_per_seq == 0)
    def _():
        carry_ref[...] = jnp.zeros_like(carry_ref)

    for c0, cw in _col_chunks(0, SHIFT_WIDTH):
        pc = jnp.dot(n_ref[...], w_ref[:, c0:c0 + cw], preferred_element_type=F32)
        row0 = lax.broadcasted_iota(jnp.int32, (tm, cw), 0) == 0
        prev = jnp.where(row0, carry_ref[0:1, c0:c0 + cw], pltpu.roll(pc, 1, 0))
        carry_ref[0:1, c0:c0 + cw] = pc[tm - 1:tm, :]
        rw_ref[:, c0:c0 + cw] = pc + (prev - pc) * mu_ref[:, c0:c0 + cw]

    base = SHIFT_WIDTH
    for dst, scale in ((ga_ref, None), (q_ref, LOG2E / math.sqrt(AT_HEAD_DIM)), (k_ref, None),
                       (v_ref, None), (gb_ref, None)):
        for c0, cw in _col_chunks(0, RW_WIDTH):
            pc = jnp.dot(n_ref[...], w_ref[:, base + c0:base + c0 + cw], preferred_element_type=F32)
            if scale is not None:
                pc = pc * scale
            dst[:, c0:c0 + cw] = pc.astype(dst.dtype)
        base += RW_WIDTH


def _in_proj_even(x2d, g, w_bf16, mu, seq_len, tm):
    t = x2d.shape[0]
    row = lambda i: (i, 0)
    const = lambda i: (0, 0)
    kern = functools.partial(_in_proj_even_kernel, tiles_per_seq=seq_len // tm)
    return pl.pallas_call(
        kern,
        grid=(t // tm,),
        in_specs=[pl.BlockSpec((tm, D_MODEL), row),
                  pl.BlockSpec((1, D_MODEL), const),
                  pl.BlockSpec((D_MODEL, EVEN_IN), const),
                  pl.BlockSpec((1, SHIFT_WIDTH), const)],
        out_specs=[pl.BlockSpec((tm, SHIFT_WIDTH), row),
                   pl.BlockSpec((tm, RW_WIDTH), row),
                   pl.BlockSpec((tm, AT_WIDTH), row),
                   pl.BlockSpec((tm, AT_WIDTH), row),
                   pl.BlockSpec((tm, AT_WIDTH), row),
                   pl.BlockSpec((tm, AT_WIDTH), row)],
        out_shape=[jax.ShapeDtypeStruct((t, SHIFT_WIDTH), F32),
                   jax.ShapeDtypeStruct((t, RW_WIDTH), F32),
                   jax.ShapeDtypeStruct((t, AT_WIDTH), BF16),
                   jax.ShapeDtypeStruct((t, AT_WIDTH), BF16),
                   jax.ShapeDtypeStruct((t, AT_WIDTH), BF16),
                   jax.ShapeDtypeStruct((t, AT_WIDTH), F32)],
        scratch_shapes=[pltpu.VMEM((tm, D_MODEL), BF16),
                        pltpu.VMEM((8, SHIFT_WIDTH), F32)],
        compiler_params=pltpu.CompilerParams(dimension_semantics=("arbitrary",),
                                             vmem_limit_bytes=VMEM_LIMIT),
        name="in_proj_even",
    )(x2d, g, w_bf16, mu)


EXP_NEG_HALF = math.exp(-0.5)


def _rwkv_kernel(rw_ref, ga_ref, w0_ref, w2_ref, a0_ref, a2_ref, kk_ref, ka_ref, rk_ref, lg_ref,
                 lb_ref, o_ref, z_ref):
    rb = rw_ref.shape[0]
    n_chunks = rb // CHUNK
    n_pairs = RW_WIDTH // PAIR

    @pl.when(pl.program_id(1) == 0)
    def _():
        z_ref[...] = jnp.zeros_like(z_ref)

    ri = lax.broadcasted_iota(jnp.int32, (PAIR, PAIR), 0)
    ci = lax.broadcasted_iota(jnp.int32, (PAIR, PAIR), 1)
    same_head = (ri // RW_HEAD_DIM) == (ci // RW_HEAD_DIM)
    tril_bd = jnp.logical_and(same_head, ci <= ri)
    stril_bd = jnp.logical_and(same_head, ci < ri)
    eye_mask = ri == ci
    eye = jnp.where(eye_mask, 1.0, 0.0).astype(F32)
    blk = {sz: (ri // sz) == (ci // sz) for sz in (8, 16, 32, 64)}
    merge = {sz: jnp.logical_and(blk[2 * sz], jnp.logical_not(blk[sz])) for sz in (8, 16, 32)}
    ones_bd = jnp.where(same_head, 1.0, 0.0).astype(BF16)
    tr = lax.broadcasted_iota(jnp.int32, (rb, rb), 0)
    tc = lax.broadcasted_iota(jnp.int32, (rb, rb), 1)
    same_chunk = (tr // CHUNK) == (tc // CHUNK)
    chunk_tril = jnp.where(jnp.logical_and(same_chunk, tc <= tr), 1.0, 0.0).astype(BF16)
    lane_head = lax.broadcasted_iota(jnp.int32, (CHUNK, PAIR), 1) // RW_HEAD_DIM
    head0 = lane_head == 0
    head1 = lane_head == 1

    def head_sum(x):
        return jnp.concatenate(
            [_dot_exact_rhs(x[:, p * PAIR:(p + 1) * PAIR], ones_bd) for p in range(n_pairs)], axis=1)

    def bd(x, cc, p):
        xs = x[cc * CHUNK:(cc + 1) * CHUNK, p * PAIR:(p + 1) * PAIR]
        return jnp.concatenate([jnp.where(head0, xs, 0.0), jnp.where(head1, xs, 0.0)], axis=0)

    r = rw_ref[:, 0:RW_WIDTH]
    k = rw_ref[:, RW_WIDTH:2 * RW_WIDTH]
    v = rw_ref[:, 2 * RW_WIDTH:3 * RW_WIDTH]
    wd = rw_ref[:, 3 * RW_WIDTH:3 * RW_WIDTH + DECAY_LORA]
    ad = rw_ref[:, 3 * RW_WIDTH + DECAY_LORA:SHIFT_WIDTH]

    zw = w0_ref[...] + _dot(jnp.tanh(wd), w2_ref[...])
    lw = -EXP_NEG_HALF * _sigmoid(zw)
    lw_hi, lw_lo = _split(lw)
    c = _dot(chunk_tril, lw_hi) + _dot(chunk_tril, lw_lo)
    c_last = jnp.concatenate(
        [jnp.broadcast_to(c[(cc + 1) * CHUNK - 1:(cc + 1) * CHUNK, :], (CHUNK, RW_WIDTH))
         for cc in range(n_chunks)], axis=0)
    e_c = jnp.exp(c)
    e_nc = jnp.exp(-c)
    e_cm = jnp.exp(c - lw)
    e_lc = jnp.exp(c_last - c)
    g_l = jnp.exp(c_last)

    a_g = _sigmoid(a0_ref[...] + _dot(ad, a2_ref[...]))
    kk = k * kk_ref[...]
    ss = head_sum(kk * kk)
    kkn = kk * lax.rsqrt(jnp.maximum(ss, 1e-24))
    k_m = k * (1.0 + (a_g - 1.0) * ka_ref[...])
    b_v = kkn * a_g
    rh = r * e_c
    kh = k_m * e_nc
    bh = b_v * e_nc
    ah = -kkn * e_cm
    bt = b_v * e_lc
    kt = k_m * e_lc
    bonus = head_sum(r * k_m * rk_ref[...]) * v

    chains = [(cc, p) for cc in range(n_chunks) for p in range(n_pairs)]
    g_mat, rh_bd, ah_bd, v_bd, bt_t, kt_t, g_last = {}, {}, {}, {}, {}, {}, {}
    for ch in chains:
        cc, p = ch
        rows = slice(cc * CHUNK, (cc + 1) * CHUNK)
        sl = slice(p * PAIR, (p + 1) * PAIR)
        rh_bd[ch] = bd(rh, cc, p)
        ah_bd[ch] = bd(ah, cc, p)
        v_bd[ch] = bd(v, cc, p)
        bt_t[ch] = bd(bt, cc, p).T
        kt_t[ch] = bd(kt, cc, p).T
        g_last[ch] = g_l[cc * CHUNK:cc * CHUNK + 1, sl]
        lhs = jnp.concatenate([rh_bd[ch], ah_bd[ch]], axis=0)
        rhs = jnp.concatenate([kh[rows, sl], kh[rows, sl], bh[rows, sl], bh[rows, sl]], axis=0)
        g_mat[ch] = _dot_nt(lhs, rhs)

    a_rk = {ch: jnp.where(tril_bd, g_mat[ch][0:PAIR, 0:PAIR], 0.0) for ch in chains}
    a_rb = {ch: jnp.where(tril_bd, g_mat[ch][0:PAIR, PAIR:2 * PAIR], 0.0) for ch in chains}
    a_ak = {ch: jnp.where(stril_bd, g_mat[ch][PAIR:2 * PAIR, 0:PAIR], 0.0) for ch in chains}
    a_ab = {ch: jnp.where(stril_bd, g_mat[ch][PAIR:2 * PAIR, PAIR:2 * PAIR], 0.0) for ch in chains}

    a8 = {ch: jnp.where(blk[8], a_ab[ch], 0.0) for ch in chains}
    a2 = {ch: _dot(a8[ch], a8[ch]) for ch in chains}
    b1 = {ch: eye + a8[ch] for ch in chains}
    b2 = {ch: b1[ch] + _dot(a2[ch], b1[ch]) for ch in chains}
    a4 = {ch: _dot(a2[ch], a2[ch]) for ch in chains}
    t_inv = {ch: b2[ch] + _dot(a4[ch], b2[ch]) for ch in chains}
    def low_rows(x, sz):
        return jnp.concatenate([x[s:s + sz] for s in range(sz, PAIR, 2 * sz)], axis=0)

    def spread_low_rows(y, sz):
        zero = jnp.zeros((sz, y.shape[1]), y.dtype)
        parts = []
        for i in range(PAIR // (2 * sz)):
            parts += [zero, y[i * sz:(i + 1) * sz]]
        return jnp.concatenate(parts, axis=0)

    for sz in (8, 16, 32):
        ed = {ch: _dot(low_rows(jnp.where(merge[sz], a_ab[ch], 0.0), sz), t_inv[ch]) for ch in chains}
        upd = {ch: _dot(low_rows(t_inv[ch], sz), spread_low_rows(ed[ch], sz)) for ch in chains}
        t_inv = {ch: t_inv[ch] + spread_low_rows(upd[ch], sz) for ch in chains}

    x1 = {ch: _dot(a_ak[ch], v_bd[ch]) for ch in chains}
    wu = {ch: _dot(t_inv[ch], jnp.concatenate([ah_bd[ch], x1[ch]], axis=1)) for ch in chains}
    zeros = jnp.zeros((PAIR, PAIR), F32)
    qymn = {}
    for ch in chains:
        lhs = jnp.concatenate([jnp.concatenate([a_rb[ch], a_rk[ch]], axis=1),
                               jnp.concatenate([bt_t[ch], kt_t[ch]], axis=1)], axis=0)
        rhs = jnp.concatenate([wu[ch], jnp.concatenate([zeros, v_bd[ch]], axis=1)], axis=0)
        qymn[ch] = _dot(lhs, rhs)

    z = [z_ref[p] for p in range(n_pairs)]
    y_rows = []
    for cc in range(n_chunks):
        y_pairs = []
        for p in range(n_pairs):
            ch = (cc, p)
            q = rh_bd[ch] + qymn[ch][0:PAIR, 0:PAIR]
            m = qymn[ch][PAIR:2 * PAIR, 0:PAIR] + jnp.where(eye_mask, g_last[ch], 0.0)
            zy = _dot(jnp.concatenate([m, q], axis=0), z[p])
            z[p] = zy[0:PAIR] + qymn[ch][PAIR:2 * PAIR, PAIR:2 * PAIR]
            y_bd = zy[PAIR:2 * PAIR] + qymn[ch][0:PAIR, PAIR:2 * PAIR]
            y_pairs.append(y_bd[0:CHUNK] + y_bd[CHUNK:PAIR])
        y_rows.append(jnp.concatenate(y_pairs, axis=1))
    for p in range(n_pairs):
        z_ref[p] = z[p]
    y = jnp.concatenate(y_rows, axis=0)
    mu = head_sum(y) * (1.0 / RW_HEAD_DIM)
    d = y - mu
    var = head_sum(d * d) * (1.0 / RW_HEAD_DIM)
    yn = d * lax.rsqrt(var + GN_EPS) * lg_ref[...] + lb_ref[...] + bonus
    o_ref[...] = (yn * _silu(ga_ref[...])).astype(o_ref.dtype)


def _rwkv(rw, ga, w0, w2, a0, a2, k_k, k_a, r_k, lnx_g, lnx_b, rb):
    b, s, _ = rw.shape
    blk3 = lambda i, j: (i, j, 0)
    const = lambda i, j: (0, 0)
    vec = pl.BlockSpec((1, RW_WIDTH), const)
    lora = pl.BlockSpec((DECAY_LORA, RW_WIDTH), const)
    return pl.pallas_call(
        _rwkv_kernel,
        grid=(b, s // rb),
        in_specs=[pl.BlockSpec((None, rb, SHIFT_WIDTH), blk3),
                  pl.BlockSpec((None, rb, RW_WIDTH), blk3),
                  vec, lora, vec, lora, vec, vec, vec, vec, vec],
        out_specs=pl.BlockSpec((None, rb, RW_WIDTH), blk3),
        out_shape=jax.ShapeDtypeStruct((b, s, RW_WIDTH), BF16),
        scratch_shapes=[pltpu.VMEM((RW_WIDTH // PAIR, PAIR, PAIR), F32)],
        compiler_params=pltpu.CompilerParams(dimension_semantics=("parallel", "arbitrary"),
                                             vmem_limit_bytes=VMEM_LIMIT),
        name="rwkv7_mix",
    )(rw, ga, w0, w2, a0, a2, k_k, k_a, r_k, lnx_g, lnx_b)


SLAB = 4 * AT_HEAD_DIM


def _attn_kernel(q_ref, k_ref, v_ref, gb_ref, bias_ref, o_ref):
    cb = q_ref.shape[0] // CHUNK
    j = pl.program_id(1)
    lane_head = lax.broadcasted_iota(jnp.int32, (CHUNK, SLAB), 1) // AT_HEAD_DIM
    head_masks = [lane_head == h for h in range(SLAB // AT_HEAD_DIM)]
    units = [(cc, s0) for cc in range(cb) for s0 in range(0, AT_WIDTH, SLAB)]

    def band(cc):
        c = j * cb + cc
        start = pl.multiple_of(jnp.maximum(c - LEFT_CHUNKS, 0) * CHUNK, CHUNK)
        boff = pl.multiple_of(jnp.maximum(LEFT_CHUNKS - c, 0) * CHUNK, CHUNK)
        return start, boff

    def scores(cc, s0):
        start, _ = band(cc)
        cols = slice(s0, s0 + SLAB)
        qs = q_ref[cc * CHUNK:(cc + 1) * CHUNK, cols]
        q_bd = jnp.concatenate([jnp.where(mh, qs, jnp.zeros_like(qs)) for mh in head_masks], axis=0)
        kb = k_ref[pl.ds(start, BAND), cols]
        return lax.dot_general(kb, q_bd, (((1,), (1,)), ((), ())), preferred_element_type=F32)

    def finish(cc, s0, st):
        start, boff = band(cc)
        rows = slice(cc * CHUNK, (cc + 1) * CHUNK)
        cols = slice(s0, s0 + SLAB)
        st = st + bias_ref[pl.ds(boff, BAND), cols]
        mx = jnp.max(st, axis=0, keepdims=True)
        e = jnp.exp2(st - mx)
        l = jnp.sum(e, axis=0, keepdims=True)
        pt = (e * (1.0 / l)).astype(BF16)
        vb = v_ref[pl.ds(start, BAND), cols]
        o_bd = lax.dot_general(pt, vb, (((0,), (0,)), ((), ())), preferred_element_type=F32)
        o = jnp.zeros((CHUNK, SLAB), F32)
        for h, mh in enumerate(head_masks):
            o = jnp.where(mh, o_bd[h * CHUNK:(h + 1) * CHUNK, :], o)
        o_ref[rows, cols] = (o * _silu(gb_ref[rows, cols])).astype(o_ref.dtype)

    st_next = scores(*units[0])
    for i, unit in enumerate(units):
        st_cur = st_next
        if i + 1 < len(units):
            st_next = scores(*units[i + 1])
        finish(*unit, st_cur)


def _attn(q, k, v, gb, bias_ext, cb):
    b, s, _ = q.shape
    qblk = lambda i, j: (i, j, 0)
    full = lambda i, j: (i, 0, 0)
    return pl.pallas_call(
        _attn_kernel,
        grid=(b, s // (cb * CHUNK)),
        in_specs=[pl.BlockSpec((None, cb * CHUNK, AT_WIDTH), qblk),
                  pl.BlockSpec((None, s, AT_WIDTH), full),
                  pl.BlockSpec((None, s, AT_WIDTH), full),
                  pl.BlockSpec((None, cb * CHUNK, AT_WIDTH), qblk),
                  pl.BlockSpec(bias_ext.shape, lambda i, j: (0, 0))],
        out_specs=pl.BlockSpec((None, cb * CHUNK, AT_WIDTH), qblk),
        out_shape=jax.ShapeDtypeStruct((b, s, AT_WIDTH), BF16),
        compiler_params=pltpu.CompilerParams(dimension_semantics=("parallel", "arbitrary"),
                                             vmem_limit_bytes=VMEM_LIMIT),
        name="chunk_attention",
    )(q, k, v, gb, bias_ext)


GMLP_SUB = 256


def _layer1_kernel(x_ref, ya_ref, yb_ref, woe_ref, g1_ref, wio_ref, lng_ref, lnb_ref, sgw_ref,
                   sgb_ref, woo_ref, fg_ref, o_ref):
    tm = x_ref.shape[0]
    gd = SG_WIDTH // SG_GROUPS
    pr = lax.broadcasted_iota(jnp.int32, (SG_CHUNK, SG_CHUNK), 0) // CHUNK
    pc = lax.broadcasted_iota(jnp.int32, (SG_CHUNK, SG_CHUNK), 1) // CHUNK
    causal = pc <= pr
    wg = [jnp.where(causal, sgw_ref[g], 0.0).astype(BF16) for g in range(SG_GROUPS)]

    subs = [slice(i * GMLP_SUB, (i + 1) * GMLP_SUB) for i in range(tm // GMLP_SUB)]
    h1 = [x_ref[rs, :] + jnp.dot(ya_ref[rs, :], woe_ref[0:RW_WIDTH, :], preferred_element_type=F32)
          + jnp.dot(yb_ref[rs, :], woe_ref[RW_WIDTH:, :], preferred_element_type=F32) for rs in subs]
    n1 = [_rmsnorm(h, g1_ref[...]).astype(BF16) for h in h1]
    vv = [_gelu_tanh(jnp.dot(n, wio_ref[:, SG_WIDTH:2 * SG_WIDTH], preferred_element_type=F32)) for n in n1]
    u = [_gelu_tanh(jnp.dot(n, wio_ref[:, 0:SG_WIDTH], preferred_element_type=F32)) for n in n1]
    gate = [jnp.dot(n, wio_ref[:, 2 * SG_WIDTH:], preferred_element_type=F32) for n in n1]
    vln = []
    for x in vv:
        mu = jnp.mean(x, axis=-1, keepdims=True)
        dv = x - mu
        var = jnp.mean(dv * dv, axis=-1, keepdims=True)
        vln.append((dv * lax.rsqrt(var + LN_EPS) * lng_ref[...] + lnb_ref[...]).astype(BF16))
    sv = []
    for xl in vln:
        cols = [jnp.concatenate([jnp.dot(wg[g], xl[nb * SG_CHUNK:(nb + 1) * SG_CHUNK, g * gd:(g + 1) * gd],
                                         preferred_element_type=F32) + sgb_ref[:, g * gd:(g + 1) * gd]
                                 for nb in range(GMLP_SUB // SG_CHUNK)], axis=0) for g in range(SG_GROUPS)]
        sv.append(jnp.concatenate(cols, axis=1))
    y = [((ui * svi) * _silu(gi)).astype(BF16) for ui, svi, gi in zip(u, sv, gate)]
    for rs, h, yi in zip(subs, h1, y):
        h2 = h + jnp.dot(yi, woo_ref[...], preferred_element_type=F32)
        o_ref[rs, :] = _rmsnorm(h2, fg_ref[...])


def _layer1(x2d, ya, yb, woe, g1, wio, lng, lnb, sgw, sgb_full, woo, fg, tm):
    t = x2d.shape[0]
    row = lambda i: (i, 0)
    const = lambda i: (0, 0)
    vec = pl.BlockSpec((1, D_MODEL), const)
    return pl.pallas_call(
        _layer1_kernel,
        grid=(t // tm,),
        in_specs=[pl.BlockSpec((tm, D_MODEL), row),
                  pl.BlockSpec((tm, RW_WIDTH), row),
                  pl.BlockSpec((tm, AT_WIDTH), row),
                  pl.BlockSpec((D_MODEL, D_MODEL), const),
                  vec,
                  pl.BlockSpec((D_MODEL, 3 * SG_WIDTH), const),
                  vec, vec,
                  pl.BlockSpec((SG_GROUPS, SG_CHUNK, SG_CHUNK), lambda i: (0, 0, 0)),
                  pl.BlockSpec((SG_CHUNK, SG_WIDTH), const),
                  pl.BlockSpec((SG_WIDTH, D_MODEL), const),
                  vec],
        out_specs=pl.BlockSpec((tm, D_MODEL), row),
        out_shape=jax.ShapeDtypeStruct((t, D_MODEL), F32),
        compiler_params=pltpu.CompilerParams(dimension_semantics=("parallel",),
                                             vmem_limit_bytes=VMEM_LIMIT),
        name="gmlp_layer",
    )(x2d, ya, yb, woe, g1, wio, lng, lnb, sgw, sgb_full, woo, fg)


def _bias_table_t(att_bias):
    n = np.arange(-(CHUNK - 1), BAND)
    idx = np.clip(LEFT_CHUNKS * CHUNK - n, -REL_CLIP, REL_CLIP) + REL_CLIP
    diag = att_bias[:, idx]
    bt = jnp.stack([diag[:, CHUNK - 1 - qi:CHUNK - 1 - qi + BAND] for qi in range(CHUNK)], axis=-1)
    bt = jnp.transpose(bt, (1, 0, 2)).reshape(BAND, -1).astype(F32) * LOG2E
    return jnp.concatenate([bt, jnp.full((LEFT_CHUNKS * CHUNK, bt.shape[1]), NEG_INF, F32)], axis=0)


def kernel(x, norm_g, w_in_e, shift_mu, rw_w0, rw_w2, rw_a0, rw_a2, rw_kk, rw_ka, rw_rk, rw_lnx_g,
           rw_lnx_b, att_bias, w_out_e, w_in_o, sg_ln_g, sg_ln_b, sg_w, sg_b, w_out_o, final_g):
    b, s, d = x.shape
    assert d == D_MODEL and s % 512 == 0 and s >= BAND
    x2d = x.reshape(b * s, d)
    row = lambda a: a.reshape(1, -1).astype(F32)

    rw, ga, q, k, v, gb = _in_proj_even(x2d, row(norm_g[0]), w_in_e[0].astype(BF16), row(shift_mu[0]),
                                        seq_len=s, tm=512)
    r3 = lambda a: a.reshape(b, s, a.shape[-1])
    ya = _rwkv(r3(rw), r3(ga), row(rw_w0[0]), rw_w2[0], row(rw_a0[0]), rw_a2[0], row(rw_kk[0]),
               row(rw_ka[0]), row(rw_rk[0]), row(rw_lnx_g[0]), row(rw_lnx_b[0]), rb=256)
    yb = _attn(r3(q), r3(k), r3(v), r3(gb), _bias_table_t(att_bias[0]), cb=4)

    sgb_full = jnp.repeat(sg_b[0].T, SG_WIDTH // SG_GROUPS, axis=1).astype(F32)
    out = _layer1(x2d, ya.reshape(b * s, -1), yb.reshape(b * s, -1), w_out_e[0].astype(BF16),
                  row(norm_g[1]), w_in_o[0].astype(BF16), row(sg_ln_g[0]), row(sg_ln_b[0]), sg_w[0],
                  sgb_full, w_out_o[0].astype(BF16), row(final_g), tm=512)
    return out.reshape(b, s, d)
```

```python
import functools
import math

import jax
import jax.numpy as jnp
import numpy as np
from jax import lax
from jax.experimental import pallas as pl
from jax.experimental.pallas import tpu as pltpu

F32 = jnp.float32
BF16 = jnp.bfloat16

D_MODEL = 1024
CHUNK = 64
RW_HEAD_DIM = 64
RW_WIDTH = 512
DECAY_LORA = 64
AAA_LORA = 64
AT_HEAD_DIM = 64
AT_WIDTH = 512
LEFT_CHUNKS = 8
BAND = (LEFT_CHUNKS + 1) * CHUNK
REL_CLIP = 2 * CHUNK
SG_CHUNK = 128
SG_WIDTH = 1024
SG_GROUPS = 8
SHIFT_WIDTH = 3 * RW_WIDTH + DECAY_LORA + AAA_LORA
EVEN_IN = SHIFT_WIDTH + RW_WIDTH + 4 * AT_WIDTH
RMS_EPS = 1e-6
LN_EPS = 1e-5
GN_EPS = 64e-5
NEG_INF = -1e30
LOG2E = math.log2(math.e)

LANES = 128
VMEM_LIMIT = 56 * 1024 * 1024

PAIR = 2 * RW_HEAD_DIM


def _dot(a, b):
    return jnp.dot(a.astype(BF16), b.astype(BF16), preferred_element_type=F32)


def _dot_nt(a, b):
    return lax.dot_general(a.astype(BF16), b.astype(BF16), (((1,), (1,)), ((), ())),
                           preferred_element_type=F32)


def _dot_tn(a, b):
    return lax.dot_general(a.astype(BF16), b.astype(BF16), (((0,), (0,)), ((), ())),
                           preferred_element_type=F32)


def _split(x):
    hi = x.astype(BF16)
    lo = (x - hi.astype(F32)).astype(BF16)
    return hi, lo


def _dot_exact_lhs(a_bf16, x):
    hi, lo = _split(x)
    return _dot(a_bf16, hi) + _dot(a_bf16, lo)


def _dot_exact_rhs(x, b_bf16):
    hi, lo = _split(x)
    return _dot(hi, b_bf16) + _dot(lo, b_bf16)


def _sigmoid(x):
    return 1.0 / (1.0 + jnp.exp(-x))


def _silu(x):
    return x * _sigmoid(x)


def _gelu_tanh(x):
    c = math.sqrt(2.0 / math.pi)
    return 0.5 * x * (1.0 + jnp.tanh(c * (x + 0.044715 * (x * x * x))))


def _rmsnorm(x, g):
    ms = jnp.mean(x * x, axis=-1, keepdims=True)
    return x * lax.rsqrt(ms + RMS_EPS) * g


def _col_chunks(lo, hi, width=256):
    out = []
    c = lo
    while c < hi:
        w = min(width, hi - c)
        out.append((c, w))
        c += w
    return out


def _in_proj_even_kernel(x_ref, g_ref, w_ref, mu_ref, rw_ref, ga_ref, q_ref, k_ref, v_ref, gb_ref,
                         n_ref, carry_ref, *, tiles_per_seq):
    tm = x_ref.shape[0]
    i = pl.program_id(0)
    n_ref[...] = _rmsnorm(x_ref[...], g_ref[...]).astype(BF16)

    @pl.when(i % tiles_per_seq == 0)
    def _():
        carry_ref[...] = jnp.zeros_like(carry_ref)

    for c0, cw in _col_chunks(0, SHIFT_WIDTH):
        pc = jnp.dot(n_ref[...], w_ref[:, c0:c0 + cw], preferred_element_type=F32)
        row0 = lax.broadcasted_iota(jnp.int32, (tm, cw), 0) == 0
        prev = jnp.where(row0, carry_ref[0:1, c0:c0 + cw], pltpu.roll(pc, 1, 0))
        carry_ref[0:1, c0:c0 + cw] = pc[tm - 1:tm, :]
        rw_ref[:, c0:c0 + cw] = pc + (prev - pc) * mu_ref[:, c0:c0 + cw]

    base = SHIFT_WIDTH
    for dst, scale in ((ga_ref, None), (q_ref, LOG2E / math.sqrt(AT_HEAD_DIM)), (k_ref, None),
                       (v_ref, None), (gb_ref, None)):
        for c0, cw in _col_chunks(0, RW_WIDTH):
            pc = jnp.dot(n_ref[...], w_ref[:, base + c0:base + c0 + cw], preferred_element_type=F32)
            if scale is not None:
                pc = pc * scale
            dst[:, c0:c0 + cw] = pc.astype(dst.dtype)
        base += RW_WIDTH


def _in_proj_even(x2d, g, w_bf16, mu, seq_len, tm):
    t = x2d.shape[0]
    row = lambda i: (i, 0)
    const = lambda i: (0, 0)
    kern = functools.partial(_in_proj_even_kernel, tiles_per_seq=seq_len // tm)
    return pl.pallas_call(
        kern,
        grid=(t // tm,),
        in_specs=[pl.BlockSpec((tm, D_MODEL), row),
                  pl.BlockSpec((1, D_MODEL), const),
                  pl.BlockSpec((D_MODEL, EVEN_IN), const),
                  pl.BlockSpec((1, SHIFT_WIDTH), const)],
        out_specs=[pl.BlockSpec((tm, SHIFT_WIDTH), row),
                   pl.BlockSpec((tm, RW_WIDTH), row),
                   pl.BlockSpec((tm, AT_WIDTH), row),
                   pl.BlockSpec((tm, AT_WIDTH), row),
                   pl.BlockSpec((tm, AT_WIDTH), row),
                   pl.BlockSpec((tm, AT_WIDTH), row)],
        out_shape=[jax.ShapeDtypeStruct((t, SHIFT_WIDTH), F32),
                   jax.ShapeDtypeStruct((t, RW_WIDTH), F32),
                   jax.ShapeDtypeStruct((t, AT_WIDTH), BF16),
                   jax.ShapeDtypeStruct((t, AT_WIDTH), BF16),
                   jax.ShapeDtypeStruct((t, AT_WIDTH), BF16),
                   jax.ShapeDtypeStruct((t, AT_WIDTH), F32)],
        scratch_shapes=[pltpu.VMEM((tm, D_MODEL), BF16),
                        pltpu.VMEM((8, SHIFT_WIDTH), F32)],
        compiler_params=pltpu.CompilerParams(dimension_semantics=("arbitrary",),
                                             vmem_limit_bytes=VMEM_LIMIT),
        name="in_proj_even",
    )(x2d, g, w_bf16, mu)


EXP_NEG_HALF = math.exp(-0.5)


def _rwkv_kernel(rw_ref, ga_ref, w0_ref, w2_ref, a0_ref, a2_ref, kk_ref, ka_ref, rk_ref, lg_ref,
                 lb_ref, o_ref, z_ref):
    rb = rw_ref.shape[0]
    n_chunks = rb // CHUNK
    n_pairs = RW_WIDTH // PAIR

    @pl.when(pl.program_id(1) == 0)
    def _():
        z_ref[...] = jnp.zeros_like(z_ref)

    ri = lax.broadcasted_iota(jnp.int32, (PAIR, PAIR), 0)
    ci = lax.broadcasted_iota(jnp.int32, (PAIR, PAIR), 1)
    same_head = (ri // RW_HEAD_DIM) == (ci // RW_HEAD_DIM)
    tril_bd = jnp.logical_and(same_head, ci <= ri)
    stril_bd = jnp.logical_and(same_head, ci < ri)
    eye_mask = ri == ci
    eye = jnp.where(eye_mask, 1.0, 0.0).astype(F32)
    blk = {sz: (ri // sz) == (ci // sz) for sz in (8, 16, 32, 64)}
    merge = {sz: jnp.logical_and(blk[2 * sz], jnp.logical_not(blk[sz])) for sz in (8, 16, 32)}
    ones_bd = jnp.where(same_head, 1.0, 0.0).astype(BF16)
    tr = lax.broadcasted_iota(jnp.int32, (rb, rb), 0)
    tc = lax.broadcasted_iota(jnp.int32, (rb, rb), 1)
    same_chunk = (tr // CHUNK) == (tc // CHUNK)
    chunk_tril = jnp.where(jnp.logical_and(same_chunk, tc <= tr), 1.0, 0.0).astype(BF16)
    lane_head = lax.broadcasted_iota(jnp.int32, (CHUNK, PAIR), 1) // RW_HEAD_DIM
    head0 = lane_head == 0
    head1 = lane_head == 1

    first_head = lax.broadcasted_iota(jnp.int32, (rb, PAIR), 1) < RW_HEAD_DIM

    def head_sum(x):
        out = []
        for p in range(n_pairs):
            xs = x[:, p * PAIR:(p + 1) * PAIR]
            s0 = jnp.sum(jnp.where(first_head, xs, 0.0), axis=1, keepdims=True)
            s1 = jnp.sum(jnp.where(first_head, 0.0, xs), axis=1, keepdims=True)
            out.append(jnp.where(first_head, s0, s1))
        return jnp.concatenate(out, axis=1)

    def bd(x, cc, p):
        xs = x[cc * CHUNK:(cc + 1) * CHUNK, p * PAIR:(p + 1) * PAIR]
        return jnp.concatenate([jnp.where(head0, xs, 0.0), jnp.where(head1, xs, 0.0)], axis=0)

    r = rw_ref[:, 0:RW_WIDTH]
    k = rw_ref[:, RW_WIDTH:2 * RW_WIDTH]
    v = rw_ref[:, 2 * RW_WIDTH:3 * RW_WIDTH]
    wd = rw_ref[:, 3 * RW_WIDTH:3 * RW_WIDTH + DECAY_LORA]
    ad = rw_ref[:, 3 * RW_WIDTH + DECAY_LORA:SHIFT_WIDTH]

    zw = w0_ref[...] + _dot(jnp.tanh(wd), w2_ref[...])
    lw = -EXP_NEG_HALF * _sigmoid(zw)
    lw_hi, lw_lo = _split(lw)
    c = _dot(chunk_tril, lw_hi) + _dot(chunk_tril, lw_lo)
    c_last = jnp.concatenate(
        [jnp.broadcast_to(c[(cc + 1) * CHUNK - 1:(cc + 1) * CHUNK, :], (CHUNK, RW_WIDTH))
         for cc in range(n_chunks)], axis=0)
    e_c = jnp.exp(c)
    e_nc = jnp.exp(-c)
    e_cm = jnp.exp(c - lw)
    e_lc = jnp.exp(c_last - c)
    g_l = jnp.exp(c_last)

    a_g = _sigmoid(a0_ref[...] + _dot(ad, a2_ref[...]))
    kk = k * kk_ref[...]
    ss = head_sum(kk * kk)
    kkn = kk * lax.rsqrt(jnp.maximum(ss, 1e-24))
    k_m = k * (1.0 + (a_g - 1.0) * ka_ref[...])
    b_v = kkn * a_g
    rh = r * e_c
    kh = k_m * e_nc
    bh = b_v * e_nc
    ah = -kkn * e_cm
    bt = b_v * e_lc
    kt = k_m * e_lc
    bonus = head_sum(r * k_m * rk_ref[...]) * v

    chains = [(cc, p) for cc in range(n_chunks) for p in range(n_pairs)]
    g_mat, rh_bd, ah_bd, v_bd, bt_t, kt_t, g_last = {}, {}, {}, {}, {}, {}, {}
    for ch in chains:
        cc, p = ch
        rows = slice(cc * CHUNK, (cc + 1) * CHUNK)
        sl = slice(p * PAIR, (p + 1) * PAIR)
        rh_bd[ch] = bd(rh, cc, p)
        ah_bd[ch] = bd(ah, cc, p)
        v_bd[ch] = bd(v, cc, p)
        bt_t[ch] = bd(bt, cc, p).T
        kt_t[ch] = bd(kt, cc, p).T
        g_last[ch] = g_l[cc * CHUNK:cc * CHUNK + 1, sl]
        lhs = jnp.concatenate([rh_bd[ch], ah_bd[ch]], axis=0)
        rhs = jnp.concatenate([kh[rows, sl], bh[rows, sl]], axis=0)
        g_mat[ch] = _dot_nt(lhs, rhs)

    g_swap = {ch: pltpu.roll(g_mat[ch], RW_HEAD_DIM, 1) for ch in chains}
    H = RW_HEAD_DIM

    def blocks(top, bottom, mask):
        return jnp.where(mask, jnp.concatenate([top, bottom], axis=0), 0.0)

    a_rk = {ch: blocks(g_mat[ch][0:H], g_swap[ch][H:2 * H], tril_bd) for ch in chains}
    a_rb = {ch: blocks(g_swap[ch][0:H], g_mat[ch][H:2 * H], tril_bd) for ch in chains}
    a_ak = {ch: blocks(g_mat[ch][2 * H:3 * H], g_swap[ch][3 * H:4 * H], stril_bd) for ch in chains}
    a_ab = {ch: blocks(g_swap[ch][2 * H:3 * H], g_mat[ch][3 * H:4 * H], stril_bd) for ch in chains}

    a8 = {ch: jnp.where(blk[8], a_ab[ch], 0.0) for ch in chains}
    a2 = {ch: _dot(a8[ch], a8[ch]) for ch in chains}
    b1 = {ch: eye + a8[ch] for ch in chains}
    b2 = {ch: b1[ch] + _dot(a2[ch], b1[ch]) for ch in chains}
    a4 = {ch: _dot(a2[ch], a2[ch]) for ch in chains}
    t_inv = {ch: b2[ch] + _dot(a4[ch], b2[ch]) for ch in chains}
    def low_rows(x, sz):
        return jnp.concatenate([x[s:s + sz] for s in range(sz, PAIR, 2 * sz)], axis=0)

    def spread_low_rows(y, sz):
        zero = jnp.zeros((sz, y.shape[1]), y.dtype)
        parts = []
        for i in range(PAIR // (2 * sz)):
            parts += [zero, y[i * sz:(i + 1) * sz]]
        return jnp.concatenate(parts, axis=0)

    for sz in (8, 16, 32):
        ed = {ch: _dot(low_rows(jnp.where(merge[sz], a_ab[ch], 0.0), sz), t_inv[ch]) for ch in chains}
        upd = {ch: _dot(low_rows(t_inv[ch], sz), spread_low_rows(ed[ch], sz)) for ch in chains}
        t_inv = {ch: t_inv[ch] + spread_low_rows(upd[ch], sz) for ch in chains}

    x1 = {ch: _dot(a_ak[ch], v_bd[ch]) for ch in chains}
    wu = {ch: _dot(t_inv[ch], jnp.concatenate([ah_bd[ch], x1[ch]], axis=1)) for ch in chains}
    zeros = jnp.zeros((PAIR, PAIR), F32)
    qymn = {}
    for ch in chains:
        lhs = jnp.concatenate([jnp.concatenate([a_rb[ch], a_rk[ch]], axis=1),
                               jnp.concatenate([bt_t[ch], kt_t[ch]], axis=1)], axis=0)
        rhs = jnp.concatenate([wu[ch], jnp.concatenate([zeros, v_bd[ch]], axis=1)], axis=0)
        qymn[ch] = _dot(lhs, rhs)

    z = [z_ref[p] for p in range(n_pairs)]
    y_rows = []
    for cc in range(n_chunks):
        y_pairs = []
        for p in range(n_pairs):
            ch = (cc, p)
            q = rh_bd[ch] + qymn[ch][0:PAIR, 0:PAIR]
            m = qymn[ch][PAIR:2 * PAIR, 0:PAIR] + jnp.where(eye_mask, g_last[ch], 0.0)
            zy = _dot(jnp.concatenate([m, q], axis=0), z[p])
            z[p] = zy[0:PAIR] + qymn[ch][PAIR:2 * PAIR, PAIR:2 * PAIR]
            y_bd = zy[PAIR:2 * PAIR] + qymn[ch][0:PAIR, PAIR:2 * PAIR]
            y_pairs.append(y_bd[0:CHUNK] + y_bd[CHUNK:PAIR])
        y_rows.append(jnp.concatenate(y_pairs, axis=1))
    for p in range(n_pairs):
        z_ref[p] = z[p]
    y = jnp.concatenate(y_rows, axis=0)
    mu = head_sum(y) * (1.0 / RW_HEAD_DIM)
    d = y - mu
    var = head_sum(d * d) * (1.0 / RW_HEAD_DIM)
    yn = d * lax.rsqrt(var + GN_EPS) * lg_ref[...] + lb_ref[...] + bonus
    o_ref[...] = (yn * _silu(ga_ref[...])).astype(o_ref.dtype)


def _rwkv(rw, ga, w0, w2, a0, a2, k_k, k_a, r_k, lnx_g, lnx_b, rb):
    b, s, _ = rw.shape
    blk3 = lambda i, j: (i, j, 0)
    const = lambda i, j: (0, 0)
    vec = pl.BlockSpec((1, RW_WIDTH), const)
    lora = pl.BlockSpec((DECAY_LORA, RW_WIDTH), const)
    return pl.pallas_call(
        _rwkv_kernel,
        grid=(b, s // rb),
        in_specs=[pl.BlockSpec((None, rb, SHIFT_WIDTH), blk3),
                  pl.BlockSpec((None, rb, RW_WIDTH), blk3),
                  vec, lora, vec, lora, vec, vec, vec, vec, vec],
        out_specs=pl.BlockSpec((None, rb, RW_WIDTH), blk3),
        out_shape=jax.ShapeDtypeStruct((b, s, RW_WIDTH), BF16),
        scratch_shapes=[pltpu.VMEM((RW_WIDTH // PAIR, PAIR, PAIR), F32)],
        compiler_params=pltpu.CompilerParams(dimension_semantics=("parallel", "arbitrary"),
                                             vmem_limit_bytes=VMEM_LIMIT),
        name="rwkv7_mix",
    )(rw, ga, w0, w2, a0, a2, k_k, k_a, r_k, lnx_g, lnx_b)


SLAB = 4 * AT_HEAD_DIM


def _attn_kernel(q_ref, k_ref, v_ref, gb_ref, bias_ref, o_ref):
    cb = q_ref.shape[0] // CHUNK
    j = pl.program_id(1)
    lane_head = lax.broadcasted_iota(jnp.int32, (CHUNK, SLAB), 1) // AT_HEAD_DIM
    head_masks = [lane_head == h for h in range(SLAB // AT_HEAD_DIM)]
    units = [(cc, s0) for cc in range(cb) for s0 in range(0, AT_WIDTH, SLAB)]

    def band(cc):
        c = j * cb + cc
        start = pl.multiple_of(jnp.maximum(c - LEFT_CHUNKS, 0) * CHUNK, CHUNK)
        boff = pl.multiple_of(jnp.maximum(LEFT_CHUNKS - c, 0) * CHUNK, CHUNK)
        return start, boff

    def scores(cc, s0):
        start, _ = band(cc)
        cols = slice(s0, s0 + SLAB)
        qs = q_ref[cc * CHUNK:(cc + 1) * CHUNK, cols]
        q_bd = jnp.concatenate([jnp.where(mh, qs, jnp.zeros_like(qs)) for mh in head_masks], axis=0)
        kb = k_ref[pl.ds(start, BAND), cols]
        return lax.dot_general(kb, q_bd, (((1,), (1,)), ((), ())), preferred_element_type=F32)

    def finish(cc, s0, st):
        start, boff = band(cc)
        rows = slice(cc * CHUNK, (cc + 1) * CHUNK)
        cols = slice(s0, s0 + SLAB)
        st = st + bias_ref[pl.ds(boff, BAND), cols]
        mx = jnp.max(st, axis=0, keepdims=True)
        e = jnp.exp2(st - mx)
        l = jnp.sum(e, axis=0, keepdims=True)
        pt = (e * (1.0 / l)).astype(BF16)
        vb = v_ref[pl.ds(start, BAND), cols]
        o_bd = lax.dot_general(pt, vb, (((0,), (0,)), ((), ())), preferred_element_type=F32)
        o = jnp.zeros((CHUNK, SLAB), F32)
        for h, mh in enumerate(head_masks):
            o = jnp.where(mh, o_bd[h * CHUNK:(h + 1) * CHUNK, :], o)
        o_ref[rows, cols] = (o * _silu(gb_ref[rows, cols])).astype(o_ref.dtype)

    st_next = scores(*units[0])
    for i, unit in enumerate(units):
        st_cur = st_next
        if i + 1 < len(units):
            st_next = scores(*units[i + 1])
        finish(*unit, st_cur)


def _attn(q, k, v, gb, bias_ext, cb):
    b, s, _ = q.shape
    qblk = lambda i, j: (i, j, 0)
    full = lambda i, j: (i, 0, 0)
    return pl.pallas_call(
        _attn_kernel,
        grid=(b, s // (cb * CHUNK)),
        in_specs=[pl.BlockSpec((None, cb * CHUNK, AT_WIDTH), qblk),
                  pl.BlockSpec((None, s, AT_WIDTH), full),
                  pl.BlockSpec((None, s, AT_WIDTH), full),
                  pl.BlockSpec((None, cb * CHUNK, AT_WIDTH), qblk),
                  pl.BlockSpec(bias_ext.shape, lambda i, j: (0, 0))],
        out_specs=pl.BlockSpec((None, cb * CHUNK, AT_WIDTH), qblk),
        out_shape=jax.ShapeDtypeStruct((b, s, AT_WIDTH), BF16),
        compiler_params=pltpu.CompilerParams(dimension_semantics=("parallel", "arbitrary"),
                                             vmem_limit_bytes=VMEM_LIMIT),
        name="chunk_attention",
    )(q, k, v, gb, bias_ext)


GMLP_SUB = 256


def _layer1_kernel(x_ref, ya_ref, yb_ref, woe_ref, g1_ref, wio_ref, lng_ref, lnb_ref, sgw_ref,
                   sgb_ref, woo_ref, fg_ref, o_ref):
    tm = x_ref.shape[0]
    gd = SG_WIDTH // SG_GROUPS
    pr = lax.broadcasted_iota(jnp.int32, (SG_CHUNK, SG_CHUNK), 0) // CHUNK
    pc = lax.broadcasted_iota(jnp.int32, (SG_CHUNK, SG_CHUNK), 1) // CHUNK
    causal = pc <= pr
    wg = [jnp.where(causal, sgw_ref[g], 0.0).astype(BF16) for g in range(SG_GROUPS)]

    subs = [slice(i * GMLP_SUB, (i + 1) * GMLP_SUB) for i in range(tm // GMLP_SUB)]
    h1 = [x_ref[rs, :] + jnp.dot(ya_ref[rs, :], woe_ref[0:RW_WIDTH, :], preferred_element_type=F32)
          + jnp.dot(yb_ref[rs, :], woe_ref[RW_WIDTH:, :], preferred_element_type=F32) for rs in subs]
    n1 = [_rmsnorm(h, g1_ref[...]).astype(BF16) for h in h1]
    vv = [_gelu_tanh(jnp.dot(n, wio_ref[:, SG_WIDTH:2 * SG_WIDTH], preferred_element_type=F32)) for n in n1]
    u = [_gelu_tanh(jnp.dot(n, wio_ref[:, 0:SG_WIDTH], preferred_element_type=F32)) for n in n1]
    gate = [jnp.dot(n, wio_ref[:, 2 * SG_WIDTH:], preferred_element_type=F32) for n in n1]
    vln = []
    for x in vv:
        mu = jnp.mean(x, axis=-1, keepdims=True)
        dv = x - mu
        var = jnp.mean(dv * dv, axis=-1, keepdims=True)
        vln.append((dv * lax.rsqrt(var + LN_EPS) * lng_ref[...] + lnb_ref[...]).astype(BF16))
    sv = []
    for xl in vln:
        cols = [jnp.concatenate([jnp.dot(wg[g], xl[nb * SG_CHUNK:(nb + 1) * SG_CHUNK, g * gd:(g + 1) * gd],
                                         preferred_element_type=F32) + sgb_ref[:, g * gd:(g + 1) * gd]
                                 for nb in range(GMLP_SUB // SG_CHUNK)], axis=0) for g in range(SG_GROUPS)]
        sv.append(jnp.concatenate(cols, axis=1))
    y = [((ui * svi) * _silu(gi)).astype(BF16) for ui, svi, gi in zip(u, sv, gate)]
    for rs, h, yi in zip(subs, h1, y):
        h2 = h + jnp.dot(yi, woo_ref[...], preferred_element_type=F32)
        o_ref[rs, :] = _rmsnorm(h2, fg_ref[...])


def _layer1(x2d, ya, yb, woe, g1, wio, lng, lnb, sgw, sgb_full, woo, fg, tm):
    t = x2d.shape[0]
    row = lambda i: (i, 0)
    const = lambda i: (0, 0)
    vec = pl.BlockSpec((1, D_MODEL), const)
    return pl.pallas_call(
        _layer1_kernel,
        grid=(t // tm,),
        in_specs=[pl.BlockSpec((tm, D_MODEL), row),
                  pl.BlockSpec((tm, RW_WIDTH), row),
                  pl.BlockSpec((tm, AT_WIDTH), row),
                  pl.BlockSpec((D_MODEL, D_MODEL), const),
                  vec,
                  pl.BlockSpec((D_MODEL, 3 * SG_WIDTH), const),
                  vec, vec,
                  pl.BlockSpec((SG_GROUPS, SG_CHUNK, SG_CHUNK), lambda i: (0, 0, 0)),
                  pl.BlockSpec((SG_CHUNK, SG_WIDTH), const),
                  pl.BlockSpec((SG_WIDTH, D_MODEL), const),
                  vec],
        out_specs=pl.BlockSpec((tm, D_MODEL), row),
        out_shape=jax.ShapeDtypeStruct((t, D_MODEL), F32),
        compiler_params=pltpu.CompilerParams(dimension_semantics=("parallel",),
                                             vmem_limit_bytes=VMEM_LIMIT),
        name="gmlp_layer",
    )(x2d, ya, yb, woe, g1, wio, lng, lnb, sgw, sgb_full, woo, fg)


def _bias_table_t(att_bias):
    n = np.arange(-(CHUNK - 1), BAND)
    idx = np.clip(LEFT_CHUNKS * CHUNK - n, -REL_CLIP, REL_CLIP) + REL_CLIP
    diag = att_bias[:, idx]
    bt = jnp.stack([diag[:, CHUNK - 1 - qi:CHUNK - 1 - qi + BAND] for qi in range(CHUNK)], axis=-1)
    bt = jnp.transpose(bt, (1, 0, 2)).reshape(BAND, -1).astype(F32) * LOG2E
    return jnp.concatenate([bt, jnp.full((LEFT_CHUNKS * CHUNK, bt.shape[1]), NEG_INF, F32)], axis=0)


def kernel(x, norm_g, w_in_e, shift_mu, rw_w0, rw_w2, rw_a0, rw_a2, rw_kk, rw_ka, rw_rk, rw_lnx_g,
           rw_lnx_b, att_bias, w_out_e, w_in_o, sg_ln_g, sg_ln_b, sg_w, sg_b, w_out_o, final_g):
    b, s, d = x.shape
    assert d == D_MODEL and s % 512 == 0 and s >= BAND
    x2d = x.reshape(b * s, d)
    row = lambda a: a.reshape(1, -1).astype(F32)

    rw, ga, q, k, v, gb = _in_proj_even(x2d, row(norm_g[0]), w_in_e[0].astype(BF16), row(shift_mu[0]),
                                        seq_len=s, tm=512)
    r3 = lambda a: a.reshape(b, s, a.shape[-1])
    ya = _rwkv(r3(rw), r3(ga), row(rw_w0[0]), rw_w2[0], row(rw_a0[0]), rw_a2[0], row(rw_kk[0]),
               row(rw_ka[0]), row(rw_rk[0]), row(rw_lnx_g[0]), row(rw_lnx_b[0]), rb=256)
    yb = _attn(r3(q), r3(k), r3(v), r3(gb), _bias_table_t(att_bias[0]), cb=8)

    sgb_full = jnp.repeat(sg_b[0].T, SG_WIDTH // SG_GROUPS, axis=1).astype(F32)
    out = _layer1(x2d, ya.reshape(b * s, -1), yb.reshape(b * s, -1), w_out_e[0].astype(BF16),
                  row(norm_g[1]), w_in_o[0].astype(BF16), row(sg_ln_g[0]), row(sg_ln_b[0]), sg_w[0],
                  sgb_full, w_out_o[0].astype(BF16), row(final_g), tm=512)
    return out.reshape(b, s, d)
```

```python
import functools
import math

import jax
import jax.numpy as jnp
import numpy as np
from jax import lax
from jax.experimental import pallas as pl
from jax.experimental.pallas import tpu as pltpu

F32 = jnp.float32
BF16 = jnp.bfloat16

D_MODEL = 1024
CHUNK = 64
RW_HEAD_DIM = 64
RW_WIDTH = 512
DECAY_LORA = 64
AAA_LORA = 64
AT_HEAD_DIM = 64
AT_WIDTH = 512
LEFT_CHUNKS = 8
BAND = (LEFT_CHUNKS + 1) * CHUNK
REL_CLIP = 2 * CHUNK
SG_CHUNK = 128
SG_WIDTH = 1024
SG_GROUPS = 8
SHIFT_WIDTH = 3 * RW_WIDTH + DECAY_LORA + AAA_LORA
EVEN_IN = SHIFT_WIDTH + RW_WIDTH + 4 * AT_WIDTH
RMS_EPS = 1e-6
LN_EPS = 1e-5
GN_EPS = 64e-5
NEG_INF = -1e30
LOG2E = math.log2(math.e)

LANES = 128
VMEM_LIMIT = 56 * 1024 * 1024

PAIR = 2 * RW_HEAD_DIM


def _dot(a, b):
    return jnp.dot(a.astype(BF16), b.astype(BF16), preferred_element_type=F32)


def _dot_nt(a, b):
    return lax.dot_general(a.astype(BF16), b.astype(BF16), (((1,), (1,)), ((), ())),
                           preferred_element_type=F32)


def _dot_tn(a, b):
    return lax.dot_general(a.astype(BF16), b.astype(BF16), (((0,), (0,)), ((), ())),
                           preferred_element_type=F32)


def _split(x):
    hi = x.astype(BF16)
    lo = (x - hi.astype(F32)).astype(BF16)
    return hi, lo


def _dot_exact_lhs(a_bf16, x):
    hi, lo = _split(x)
    return _dot(a_bf16, hi) + _dot(a_bf16, lo)


def _dot_exact_rhs(x, b_bf16):
    hi, lo = _split(x)
    return _dot(hi, b_bf16) + _dot(lo, b_bf16)


def _sigmoid(x):
    return 1.0 / (1.0 + jnp.exp(-x))


def _silu(x):
    return x * _sigmoid(x)


def _gelu_tanh(x):
    c = math.sqrt(2.0 / math.pi)
    return 0.5 * x * (1.0 + jnp.tanh(c * (x + 0.044715 * (x * x * x))))


def _rmsnorm(x, g):
    ms = jnp.mean(x * x, axis=-1, keepdims=True)
    return x * lax.rsqrt(ms + RMS_EPS) * g


def _col_chunks(lo, hi, width=256):
    out = []
    c = lo
    while c < hi:
        w = min(width, hi - c)
        out.append((c, w))
        c += w
    return out


def _in_proj_even_kernel(x_ref, g_ref, w_ref, mu_ref, rw_ref, ga_ref, q_ref, k_ref, v_ref, gb_ref,
                         n_ref, carry_ref, *, tiles_per_seq):
    tm = x_ref.shape[0]
    i = pl.program_id(0)
    n_ref[...] = _rmsnorm(x_ref[...], g_ref[...]).astype(BF16)

    @pl.when(i % tiles_per_seq == 0)
    def _():
        carry_ref[...] = jnp.zeros_like(carry_ref)

    for c0, cw in _col_chunks(0, SHIFT_WIDTH):
        pc = jnp.dot(n_ref[...], w_ref[:, c0:c0 + cw], preferred_element_type=F32)
        row0 = lax.broadcasted_iota(jnp.int32, (tm, cw), 0) == 0
        prev = jnp.where(row0, carry_ref[0:1, c0:c0 + cw], pltpu.roll(pc, 1, 0))
        carry_ref[0:1, c0:c0 + cw] = pc[tm - 1:tm, :]
        rw_ref[:, c0:c0 + cw] = pc + (prev - pc) * mu_ref[:, c0:c0 + cw]

    base = SHIFT_WIDTH
    for dst, scale in ((ga_ref, None), (q_ref, LOG2E / math.sqrt(AT_HEAD_DIM)), (k_ref, None),
                       (v_ref, None), (gb_ref, None)):
        for c0, cw in _col_chunks(0, RW_WIDTH):
            pc = jnp.dot(n_ref[...], w_ref[:, base + c0:base + c0 + cw], preferred_element_type=F32)
            if scale is not None:
                pc = pc * scale
            dst[:, c0:c0 + cw] = pc.astype(dst.dtype)
        base += RW_WIDTH


def _in_proj_even(x2d, g, w_bf16, mu, seq_len, tm):
    t = x2d.shape[0]
    row = lambda i: (i, 0)
    const = lambda i: (0, 0)
    kern = functools.partial(_in_proj_even_kernel, tiles_per_seq=seq_len // tm)
    return pl.pallas_call(
        kern,
        grid=(t // tm,),
        in_specs=[pl.BlockSpec((tm, D_MODEL), row),
                  pl.BlockSpec((1, D_MODEL), const),
                  pl.BlockSpec((D_MODEL, EVEN_IN), const),
                  pl.BlockSpec((1, SHIFT_WIDTH), const)],
        out_specs=[pl.BlockSpec((tm, SHIFT_WIDTH), row),
                   pl.BlockSpec((tm, RW_WIDTH), row),
                   pl.BlockSpec((tm, AT_WIDTH), row),
                   pl.BlockSpec((tm, AT_WIDTH), row),
                   pl.BlockSpec((tm, AT_WIDTH), row),
                   pl.BlockSpec((tm, AT_WIDTH), row)],
        out_shape=[jax.ShapeDtypeStruct((t, SHIFT_WIDTH), F32),
                   jax.ShapeDtypeStruct((t, RW_WIDTH), F32),
                   jax.ShapeDtypeStruct((t, AT_WIDTH), BF16),
                   jax.ShapeDtypeStruct((t, AT_WIDTH), BF16),
                   jax.ShapeDtypeStruct((t, AT_WIDTH), BF16),
                   jax.ShapeDtypeStruct((t, AT_WIDTH), F32)],
        scratch_shapes=[pltpu.VMEM((tm, D_MODEL), BF16),
                        pltpu.VMEM((8, SHIFT_WIDTH), F32)],
        compiler_params=pltpu.CompilerParams(dimension_semantics=("arbitrary",),
                                             vmem_limit_bytes=VMEM_LIMIT),
        name="in_proj_even",
    )(x2d, g, w_bf16, mu)


EXP_NEG_HALF = math.exp(-0.5)


def _rwkv_kernel(rw_ref, ga_ref, w0_ref, w2_ref, a0_ref, a2_ref, kk_ref, ka_ref, rk_ref, lg_ref,
                 lb_ref, o_ref, z_ref):
    rb = rw_ref.shape[0]
    n_chunks = rb // CHUNK
    n_pairs = RW_WIDTH // PAIR

    @pl.when(pl.program_id(1) == 0)
    def _():
        z_ref[...] = jnp.zeros_like(z_ref)

    ri = lax.broadcasted_iota(jnp.int32, (PAIR, PAIR), 0)
    ci = lax.broadcasted_iota(jnp.int32, (PAIR, PAIR), 1)
    same_head = (ri // RW_HEAD_DIM) == (ci // RW_HEAD_DIM)
    tril_bd = jnp.logical_and(same_head, ci <= ri)
    stril_bd = jnp.logical_and(same_head, ci < ri)
    eye_mask = ri == ci
    eye = jnp.where(eye_mask, 1.0, 0.0).astype(F32)
    blk = {sz: (ri // sz) == (ci // sz) for sz in (8, 16, 32, 64)}
    merge = {sz: jnp.logical_and(blk[2 * sz], jnp.logical_not(blk[sz])) for sz in (8, 16, 32)}
    ones_bd = jnp.where(same_head, 1.0, 0.0).astype(BF16)
    tr = lax.broadcasted_iota(jnp.int32, (rb, rb), 0)
    tc = lax.broadcasted_iota(jnp.int32, (rb, rb), 1)
    same_chunk = (tr // CHUNK) == (tc // CHUNK)
    chunk_tril = jnp.where(jnp.logical_and(same_chunk, tc <= tr), 1.0, 0.0).astype(BF16)
    lane_head = lax.broadcasted_iota(jnp.int32, (CHUNK, PAIR), 1) // RW_HEAD_DIM
    head0 = lane_head == 0
    head1 = lane_head == 1

    first_head = lax.broadcasted_iota(jnp.int32, (rb, PAIR), 1) < RW_HEAD_DIM

    def head_sum(x):
        out = []
        for p in range(n_pairs):
            xs = x[:, p * PAIR:(p + 1) * PAIR]
            s0 = jnp.sum(jnp.where(first_head, xs, 0.0), axis=1, keepdims=True)
            s1 = jnp.sum(jnp.where(first_head, 0.0, xs), axis=1, keepdims=True)
            out.append(jnp.where(first_head, s0, s1))
        return jnp.concatenate(out, axis=1)

    def bd(x, cc, p):
        xs = x[cc * CHUNK:(cc + 1) * CHUNK, p * PAIR:(p + 1) * PAIR]
        return jnp.concatenate([jnp.where(head0, xs, 0.0), jnp.where(head1, xs, 0.0)], axis=0)

    r = rw_ref[:, 0:RW_WIDTH]
    k = rw_ref[:, RW_WIDTH:2 * RW_WIDTH]
    v = rw_ref[:, 2 * RW_WIDTH:3 * RW_WIDTH]
    wd = rw_ref[:, 3 * RW_WIDTH:3 * RW_WIDTH + DECAY_LORA]
    ad = rw_ref[:, 3 * RW_WIDTH + DECAY_LORA:SHIFT_WIDTH]

    zw = w0_ref[...] + _dot(jnp.tanh(wd), w2_ref[...])
    lw = -EXP_NEG_HALF * _sigmoid(zw)
    lw_hi, lw_lo = _split(lw)
    c = _dot(chunk_tril, lw_hi) + _dot(chunk_tril, lw_lo)
    c_last = jnp.concatenate(
        [jnp.broadcast_to(c[(cc + 1) * CHUNK - 1:(cc + 1) * CHUNK, :], (CHUNK, RW_WIDTH))
         for cc in range(n_chunks)], axis=0)
    e_c = jnp.exp(c)
    e_nc = jnp.exp(-c)
    e_cm = jnp.exp(c - lw)
    e_lc = jnp.exp(c_last - c)
    g_l = jnp.exp(c_last)

    a_g = _sigmoid(a0_ref[...] + _dot(ad, a2_ref[...]))
    kk = k * kk_ref[...]
    ss = head_sum(kk * kk)
    kkn = kk * lax.rsqrt(jnp.maximum(ss, 1e-24))
    k_m = k * (1.0 + (a_g - 1.0) * ka_ref[...])
    b_v = kkn * a_g
    rh = r * e_c
    kh = k_m * e_nc
    bh = b_v * e_nc
    ah = -kkn * e_cm
    bt = b_v * e_lc
    kt = k_m * e_lc
    bonus = head_sum(r * k_m * rk_ref[...]) * v

    chains = [(cc, p) for cc in range(n_chunks) for p in range(n_pairs)]
    g_mat, rh_bd, ah_bd, v_bd, bt_t, kt_t, g_last = {}, {}, {}, {}, {}, {}, {}
    for ch in chains:
        cc, p = ch
        rows = slice(cc * CHUNK, (cc + 1) * CHUNK)
        sl = slice(p * PAIR, (p + 1) * PAIR)
        rh_bd[ch] = bd(rh, cc, p)
        ah_bd[ch] = bd(ah, cc, p)
        v_bd[ch] = bd(v, cc, p)
        bt_t[ch] = bd(bt, cc, p).T
        kt_t[ch] = bd(kt, cc, p).T
        g_last[ch] = g_l[cc * CHUNK:cc * CHUNK + 1, sl]
        lhs = jnp.concatenate([rh_bd[ch], ah_bd[ch]], axis=0)
        rhs = jnp.concatenate([kh[rows, sl], bh[rows, sl]], axis=0)
        g_mat[ch] = _dot_nt(lhs, rhs)

    g_swap = {ch: pltpu.roll(g_mat[ch], RW_HEAD_DIM, 1) for ch in chains}
    H = RW_HEAD_DIM

    def blocks(top, bottom, mask):
        return jnp.where(mask, jnp.concatenate([top, bottom], axis=0), 0.0)

    a_rk = {ch: blocks(g_mat[ch][0:H], g_swap[ch][H:2 * H], tril_bd) for ch in chains}
    a_rb = {ch: blocks(g_swap[ch][0:H], g_mat[ch][H:2 * H], tril_bd) for ch in chains}
    a_ak = {ch: blocks(g_mat[ch][2 * H:3 * H], g_swap[ch][3 * H:4 * H], stril_bd) for ch in chains}
    a_ab = {ch: blocks(g_swap[ch][2 * H:3 * H], g_mat[ch][3 * H:4 * H], stril_bd) for ch in chains}

    a8 = {ch: jnp.where(blk[8], a_ab[ch], 0.0) for ch in chains}
    a2 = {ch: _dot(a8[ch], a8[ch]) for ch in chains}
    b1 = {ch: eye + a8[ch] for ch in chains}
    b2 = {ch: b1[ch] + _dot(a2[ch], b1[ch]) for ch in chains}
    a4 = {ch: _dot(a2[ch], a2[ch]) for ch in chains}
    t_inv = {ch: b2[ch] + _dot(a4[ch], b2[ch]) for ch in chains}
    def low_rows(x, sz):
        return jnp.concatenate([x[s:s + sz] for s in range(sz, PAIR, 2 * sz)], axis=0)

    def spread_low_rows(y, sz):
        zero = jnp.zeros((sz, y.shape[1]), y.dtype)
        parts = []
        for i in range(PAIR // (2 * sz)):
            parts += [zero, y[i * sz:(i + 1) * sz]]
        return jnp.concatenate(parts, axis=0)

    for sz in (8, 16, 32):
        ed = {ch: _dot(low_rows(jnp.where(merge[sz], a_ab[ch], 0.0), sz), t_inv[ch]) for ch in chains}
        upd = {ch: _dot(low_rows(t_inv[ch], sz), spread_low_rows(ed[ch], sz)) for ch in chains}
        t_inv = {ch: t_inv[ch] + spread_low_rows(upd[ch], sz) for ch in chains}

    x1 = {ch: _dot(a_ak[ch], v_bd[ch]) for ch in chains}
    wu = {ch: _dot(t_inv[ch], jnp.concatenate([ah_bd[ch], x1[ch]], axis=1)) for ch in chains}
    zeros = jnp.zeros((PAIR, PAIR), F32)
    qymn = {}
    for ch in chains:
        lhs = jnp.concatenate([jnp.concatenate([a_rb[ch], a_rk[ch]], axis=1),
                               jnp.concatenate([bt_t[ch], kt_t[ch]], axis=1)], axis=0)
        rhs = jnp.concatenate([wu[ch], jnp.concatenate([zeros, v_bd[ch]], axis=1)], axis=0)
        qymn[ch] = _dot(lhs, rhs)

    z = [z_ref[p] for p in range(n_pairs)]
    y_rows = []
    for cc in range(n_chunks):
        y_pairs = []
        for p in range(n_pairs):
            ch = (cc, p)
            q = rh_bd[ch] + qymn[ch][0:PAIR, 0:PAIR]
            m = qymn[ch][PAIR:2 * PAIR, 0:PAIR] + jnp.where(eye_mask, g_last[ch], 0.0)
            zy = _dot(jnp.concatenate([m, q], axis=0), z[p])
            z[p] = zy[0:PAIR] + qymn[ch][PAIR:2 * PAIR, PAIR:2 * PAIR]
            y_bd = zy[PAIR:2 * PAIR] + qymn[ch][0:PAIR, PAIR:2 * PAIR]
            y_pairs.append(y_bd[0:CHUNK] + y_bd[CHUNK:PAIR])
        y_rows.append(jnp.concatenate(y_pairs, axis=1))
    for p in range(n_pairs):
        z_ref[p] = z[p]
    y = jnp.concatenate(y_rows, axis=0)
    mu = head_sum(y) * (1.0 / RW_HEAD_DIM)
    d = y - mu
    var = head_sum(d * d) * (1.0 / RW_HEAD_DIM)
    yn = d * lax.rsqrt(var + GN_EPS) * lg_ref[...] + lb_ref[...] + bonus
    o_ref[...] = (yn * _silu(ga_ref[...])).astype(o_ref.dtype)


def _rwkv(rw, ga, w0, w2, a0, a2, k_k, k_a, r_k, lnx_g, lnx_b, rb):
    b, s, _ = rw.shape
    blk3 = lambda i, j: (i, j, 0)
    const = lambda i, j: (0, 0)
    vec = pl.BlockSpec((1, RW_WIDTH), const)
    lora = pl.BlockSpec((DECAY_LORA, RW_WIDTH), const)
    return pl.pallas_call(
        _rwkv_kernel,
        grid=(b, s // rb),
        in_specs=[pl.BlockSpec((None, rb, SHIFT_WIDTH), blk3),
                  pl.BlockSpec((None, rb, RW_WIDTH), blk3),
                  vec, lora, vec, lora, vec, vec, vec, vec, vec],
        out_specs=pl.BlockSpec((None, rb, RW_WIDTH), blk3),
        out_shape=jax.ShapeDtypeStruct((b, s, RW_WIDTH), BF16),
        scratch_shapes=[pltpu.VMEM((RW_WIDTH // PAIR, PAIR, PAIR), F32)],
        compiler_params=pltpu.CompilerParams(dimension_semantics=("parallel", "arbitrary"),
                                             vmem_limit_bytes=VMEM_LIMIT),
        name="rwkv7_mix",
    )(rw, ga, w0, w2, a0, a2, k_k, k_a, r_k, lnx_g, lnx_b)


def _rwkv_pipelined_kernel(rw_ref, ga_ref, w0_ref, w2_ref, a0_ref, a2_ref, kk_ref, ka_ref, rk_ref,
                           lg_ref, lb_ref, o_ref, z_ref):
    nb, rb = rw_ref.shape[0], rw_ref.shape[1]
    n_chunks = rb // CHUNK
    n_pairs = RW_WIDTH // PAIR
    H = RW_HEAD_DIM

    @pl.when(pl.program_id(1) == 0)
    def _():
        z_ref[...] = jnp.zeros_like(z_ref)

    ri = lax.broadcasted_iota(jnp.int32, (PAIR, PAIR), 0)
    ci = lax.broadcasted_iota(jnp.int32, (PAIR, PAIR), 1)
    same_head = (ri // H) == (ci // H)
    tril_bd = jnp.logical_and(same_head, ci <= ri)
    stril_bd = jnp.logical_and(same_head, ci < ri)
    eye_mask = ri == ci
    eye = jnp.where(eye_mask, 1.0, 0.0).astype(F32)
    blk = {sz: (ri // sz) == (ci // sz) for sz in (8, 16, 32, 64)}
    merge = {sz: jnp.logical_and(blk[2 * sz], jnp.logical_not(blk[sz])) for sz in (8, 16, 32)}
    tr = lax.broadcasted_iota(jnp.int32, (rb, rb), 0)
    tc = lax.broadcasted_iota(jnp.int32, (rb, rb), 1)
    chunk_tril = jnp.where(jnp.logical_and((tr // CHUNK) == (tc // CHUNK), tc <= tr), 1.0, 0.0).astype(BF16)
    lane_head = lax.broadcasted_iota(jnp.int32, (CHUNK, PAIR), 1) // H
    head0 = lane_head == 0
    head1 = lane_head == 1
    first_head = lax.broadcasted_iota(jnp.int32, (rb, PAIR), 1) < H
    zeros = jnp.zeros((PAIR, PAIR), F32)

    def head_sum(x):
        out = []
        for p in range(n_pairs):
            xs = x[:, p * PAIR:(p + 1) * PAIR]
            s0 = jnp.sum(jnp.where(first_head, xs, 0.0), axis=1, keepdims=True)
            s1 = jnp.sum(jnp.where(first_head, 0.0, xs), axis=1, keepdims=True)
            out.append(jnp.where(first_head, s0, s1))
        return jnp.concatenate(out, axis=1)

    def bd(x, cc, p):
        xs = x[cc * CHUNK:(cc + 1) * CHUNK, p * PAIR:(p + 1) * PAIR]
        return jnp.concatenate([jnp.where(head0, xs, 0.0), jnp.where(head1, xs, 0.0)], axis=0)

    def low_rows(x, sz):
        return jnp.concatenate([x[s:s + sz] for s in range(sz, PAIR, 2 * sz)], axis=0)

    def spread_low_rows(y, sz):
        zero = jnp.zeros((sz, y.shape[1]), y.dtype)
        parts = []
        for i in range(PAIR // (2 * sz)):
            parts += [zero, y[i * sz:(i + 1) * sz]]
        return jnp.concatenate(parts, axis=0)

    def blocks(top, bottom, mask):
        return jnp.where(mask, jnp.concatenate([top, bottom], axis=0), 0.0)

    chains = [(cc, p) for cc in range(n_chunks) for p in range(n_pairs)]
    st = [dict() for _ in range(nb)]

    def prep_lora(bi):
        s = st[bi]
        wd = rw_ref[bi, :, 3 * RW_WIDTH:3 * RW_WIDTH + DECAY_LORA]
        ad = rw_ref[bi, :, 3 * RW_WIDTH + DECAY_LORA:SHIFT_WIDTH]
        s["zw"] = w0_ref[...] + _dot(jnp.tanh(wd), w2_ref[...])
        s["za"] = a0_ref[...] + _dot(ad, a2_ref[...])

    def prep_decay(bi):
        s = st[bi]
        lw = -EXP_NEG_HALF * _sigmoid(s["zw"])
        lw_hi, lw_lo = _split(lw)
        s["lw"] = lw
        s["c"] = _dot(chunk_tril, lw_hi) + _dot(chunk_tril, lw_lo)

    def prep_rows(bi):
        s = st[bi]
        r = rw_ref[bi, :, 0:RW_WIDTH]
        k = rw_ref[bi, :, RW_WIDTH:2 * RW_WIDTH]
        v = rw_ref[bi, :, 2 * RW_WIDTH:3 * RW_WIDTH]
        c, lw = s["c"], s["lw"]
        c_last = jnp.concatenate(
            [jnp.broadcast_to(c[(cc + 1) * CHUNK - 1:(cc + 1) * CHUNK, :], (CHUNK, RW_WIDTH))
             for cc in range(n_chunks)], axis=0)
        e_c = jnp.exp(c)
        e_nc = jnp.exp(-c)
        e_cm = jnp.exp(c - lw)
        e_lc = jnp.exp(c_last - c)
        g_l = jnp.exp(c_last)
        a_g = _sigmoid(s["za"])
        kk = k * kk_ref[...]
        ss = head_sum(kk * kk)
        kkn = kk * lax.rsqrt(jnp.maximum(ss, 1e-24))
        k_m = k * (1.0 + (a_g - 1.0) * ka_ref[...])
        b_v = kkn * a_g
        rh = r * e_c
        kh = k_m * e_nc
        bh = b_v * e_nc
        ah = -kkn * e_cm
        bt = b_v * e_lc
        kt = k_m * e_lc
        s["bonus"] = head_sum(r * k_m * rk_ref[...]) * v
        for key in ("rh_bd", "ah_bd", "v_bd", "bt_t", "kt_t", "g_last", "g"):
            s[key] = {}
        for ch in chains:
            cc, p = ch
            rows = slice(cc * CHUNK, (cc + 1) * CHUNK)
            sl = slice(p * PAIR, (p + 1) * PAIR)
            s["rh_bd"][ch] = bd(rh, cc, p)
            s["ah_bd"][ch] = bd(ah, cc, p)
            s["v_bd"][ch] = bd(v, cc, p)
            s["bt_t"][ch] = bd(bt, cc, p).T
            s["kt_t"][ch] = bd(kt, cc, p).T
            s["g_last"][ch] = g_l[cc * CHUNK:cc * CHUNK + 1, sl]
            lhs = jnp.concatenate([s["rh_bd"][ch], s["ah_bd"][ch]], axis=0)
            rhs = jnp.concatenate([kh[rows, sl], bh[rows, sl]], axis=0)
            s["g"][ch] = _dot_nt(lhs, rhs)

    def m_blocks(bi):
        s = st[bi]
        g = s.pop("g")
        gs = {ch: pltpu.roll(g[ch], H, 1) for ch in chains}
        s["a_rk"] = {ch: blocks(g[ch][0:H], gs[ch][H:2 * H], tril_bd) for ch in chains}
        s["a_rb"] = {ch: blocks(gs[ch][0:H], g[ch][H:2 * H], tril_bd) for ch in chains}
        s["a_ak"] = {ch: blocks(g[ch][2 * H:3 * H], gs[ch][3 * H:4 * H], stril_bd) for ch in chains}
        s["a_ab"] = {ch: blocks(gs[ch][2 * H:3 * H], g[ch][3 * H:4 * H], stril_bd) for ch in chains}
        s["a8"] = {ch: jnp.where(blk[8], s["a_ab"][ch], 0.0) for ch in chains}
        s["a2"] = {ch: _dot(s["a8"][ch], s["a8"][ch]) for ch in chains}

    def m_neumann2(bi):
        s = st[bi]
        b1 = {ch: eye + s["a8"][ch] for ch in chains}
        s["b2"] = {ch: b1[ch] + _dot(s["a2"][ch], b1[ch]) for ch in chains}
        s["a4"] = {ch: _dot(s["a2"][ch], s["a2"][ch]) for ch in chains}

    def m_neumann3(bi):
        s = st[bi]
        s["t"] = {ch: s["b2"][ch] + _dot(s["a4"][ch], s["b2"][ch]) for ch in chains}

    def m_merge_a(sz):
        def f(bi):
            s = st[bi]
            s["ed"] = {ch: _dot(low_rows(jnp.where(merge[sz], s["a_ab"][ch], 0.0), sz), s["t"][ch])
                       for ch in chains}
        return f

    def m_merge_b(sz):
        def f(bi):
            s = st[bi]
            upd = {ch: _dot(low_rows(s["t"][ch], sz), spread_low_rows(s["ed"][ch], sz)) for ch in chains}
            s["t"] = {ch: s["t"][ch] + spread_low_rows(upd[ch], sz) for ch in chains}
        return f

    def m_x1(bi):
        s = st[bi]
        s["x1"] = {ch: _dot(s["a_ak"][ch], s["v_bd"][ch]) for ch in chains}

    def m_wu(bi):
        s = st[bi]
        s["wu"] = {ch: _dot(s["t"][ch], jnp.concatenate([s["ah_bd"][ch], s["x1"][ch]], axis=1))
                   for ch in chains}

    def m_qymn(bi):
        s = st[bi]
        s["qymn"] = {}
        for ch in chains:
            lhs = jnp.concatenate([jnp.concatenate([s["a_rb"][ch], s["a_rk"][ch]], axis=1),
                                   jnp.concatenate([s["bt_t"][ch], s["kt_t"][ch]], axis=1)], axis=0)
            rhs = jnp.concatenate([s["wu"][ch], jnp.concatenate([zeros, s["v_bd"][ch]], axis=1)], axis=0)
            s["qymn"][ch] = _dot(lhs, rhs)
        s["z"] = [z_ref[bi, p] for p in range(n_pairs)]
        s["y_rows"] = []

    def tail(cc):
        def f(bi):
            s = st[bi]
            y_pairs = []
            for p in range(n_pairs):
                ch = (cc, p)
                qymn = s["qymn"][ch]
                q = s["rh_bd"][ch] + qymn[0:PAIR, 0:PAIR]
                m = qymn[PAIR:2 * PAIR, 0:PAIR] + jnp.where(eye_mask, s["g_last"][ch], 0.0)
                zy = _dot(jnp.concatenate([m, q], axis=0), s["z"][p])
                s["z"][p] = zy[0:PAIR] + qymn[PAIR:2 * PAIR, PAIR:2 * PAIR]
                y_bd = zy[PAIR:2 * PAIR] + qymn[0:PAIR, PAIR:2 * PAIR]
                y_pairs.append(y_bd[0:CHUNK] + y_bd[CHUNK:PAIR])
            s["y_rows"].append(jnp.concatenate(y_pairs, axis=1))
        return f

    def epilogue(bi):
        s = st[bi]
        for p in range(n_pairs):
            z_ref[bi, p] = s["z"][p]
        y = jnp.concatenate(s["y_rows"], axis=0)
        mu = head_sum(y) * (1.0 / H)
        d = y - mu
        var = head_sum(d * d) * (1.0 / H)
        yn = d * lax.rsqrt(var + GN_EPS) * lg_ref[...] + lb_ref[...] + s["bonus"]
        o_ref[bi] = (yn * _silu(ga_ref[bi])).astype(o_ref.dtype)
        s.clear()

    prep = [prep_lora, prep_decay, prep_rows]
    middle = [m_blocks, m_neumann2, m_neumann3]
    for sz in (8, 16, 32):
        middle += [m_merge_a(sz), m_merge_b(sz)]
    middle += [m_x1, m_wu, m_qymn]
    finish = [tail(cc) for cc in range(n_chunks)] + [epilogue]

    for f in prep:
        f(0)
    for bi in range(nb):
        side = []
        if bi > 0:
            side += [(g, bi - 1) for g in finish]
        if bi + 1 < nb:
            side += [(g, bi + 1) for g in prep]
        every = max(1, len(middle) // (len(side) + 1)) if side else 0
        for i, f in enumerate(middle):
            f(bi)
            if side and (i + 1) % every == 0:
                g, other = side.pop(0)
                g(other)
        for g, other in side:
            g(other)
    for g in finish:
        g(nb - 1)


def _rwkv_pipelined(rw, ga, w0, w2, a0, a2, k_k, k_a, r_k, lnx_g, lnx_b, nb, rb):
    b, s, _ = rw.shape
    blk3 = lambda i, j: (i, j, 0)
    const = lambda i, j: (0, 0)
    vec = pl.BlockSpec((1, RW_WIDTH), const)
    lora = pl.BlockSpec((DECAY_LORA, RW_WIDTH), const)
    return pl.pallas_call(
        _rwkv_pipelined_kernel,
        grid=(b // nb, s // rb),
        in_specs=[pl.BlockSpec((nb, rb, SHIFT_WIDTH), blk3),
                  pl.BlockSpec((nb, rb, RW_WIDTH), blk3),
                  vec, lora, vec, lora, vec, vec, vec, vec, vec],
        out_specs=pl.BlockSpec((nb, rb, RW_WIDTH), blk3),
        out_shape=jax.ShapeDtypeStruct((b, s, RW_WIDTH), BF16),
        scratch_shapes=[pltpu.VMEM((nb, RW_WIDTH // PAIR, PAIR, PAIR), F32)],
        compiler_params=pltpu.CompilerParams(dimension_semantics=("parallel", "arbitrary"),
                                             vmem_limit_bytes=VMEM_LIMIT),
        name="rwkv7_mix",
    )(rw, ga, w0, w2, a0, a2, k_k, k_a, r_k, lnx_g, lnx_b)


SLAB = 4 * AT_HEAD_DIM


def _attn_kernel(q_ref, k_ref, v_ref, gb_ref, bias_ref, o_ref):
    cb = q_ref.shape[0] // CHUNK
    j = pl.program_id(1)
    lane_head = lax.broadcasted_iota(jnp.int32, (CHUNK, SLAB), 1) // AT_HEAD_DIM
    head_masks = [lane_head == h for h in range(SLAB // AT_HEAD_DIM)]
    units = [(cc, s0) for cc in range(cb) for s0 in range(0, AT_WIDTH, SLAB)]

    def band(cc):
        c = j * cb + cc
        start = pl.multiple_of(jnp.maximum(c - LEFT_CHUNKS, 0) * CHUNK, CHUNK)
        boff = pl.multiple_of(jnp.maximum(LEFT_CHUNKS - c, 0) * CHUNK, CHUNK)
        return start, boff

    def scores(cc, s0):
        start, _ = band(cc)
        cols = slice(s0, s0 + SLAB)
        qs = q_ref[cc * CHUNK:(cc + 1) * CHUNK, cols]
        q_bd = jnp.concatenate([jnp.where(mh, qs, jnp.zeros_like(qs)) for mh in head_masks], axis=0)
        kb = k_ref[pl.ds(start, BAND), cols]
        return lax.dot_general(kb, q_bd, (((1,), (1,)), ((), ())), preferred_element_type=F32)

    def finish(cc, s0, st):
        start, boff = band(cc)
        rows = slice(cc * CHUNK, (cc + 1) * CHUNK)
        cols = slice(s0, s0 + SLAB)
        st = st + bias_ref[pl.ds(boff, BAND), cols]
        mx = jnp.max(st, axis=0, keepdims=True)
        e = jnp.exp2(st - mx)
        l = jnp.sum(e, axis=0, keepdims=True)
        pt = (e * (1.0 / l)).astype(BF16)
        vb = v_ref[pl.ds(start, BAND), cols]
        o_bd = lax.dot_general(pt, vb, (((0,), (0,)), ((), ())), preferred_element_type=F32)
        o = jnp.zeros((CHUNK, SLAB), F32)
        for h, mh in enumerate(head_masks):
            o = jnp.where(mh, o_bd[h * CHUNK:(h + 1) * CHUNK, :], o)
        o_ref[rows, cols] = (o * _silu(gb_ref[rows, cols])).astype(o_ref.dtype)

    st_next = scores(*units[0])
    for i, unit in enumerate(units):
        st_cur = st_next
        if i + 1 < len(units):
            st_next = scores(*units[i + 1])
        finish(*unit, st_cur)


def _attn(q, k, v, gb, bias_ext, cb):
    b, s, _ = q.shape
    qblk = lambda i, j: (i, j, 0)
    full = lambda i, j: (i, 0, 0)
    return pl.pallas_call(
        _attn_kernel,
        grid=(b, s // (cb * CHUNK)),
        in_specs=[pl.BlockSpec((None, cb * CHUNK, AT_WIDTH), qblk),
                  pl.BlockSpec((None, s, AT_WIDTH), full),
                  pl.BlockSpec((None, s, AT_WIDTH), full),
                  pl.BlockSpec((None, cb * CHUNK, AT_WIDTH), qblk),
                  pl.BlockSpec(bias_ext.shape, lambda i, j: (0, 0))],
        out_specs=pl.BlockSpec((None, cb * CHUNK, AT_WIDTH), qblk),
        out_shape=jax.ShapeDtypeStruct((b, s, AT_WIDTH), BF16),
        compiler_params=pltpu.CompilerParams(dimension_semantics=("parallel", "arbitrary"),
                                             vmem_limit_bytes=VMEM_LIMIT),
        name="chunk_attention",
    )(q, k, v, gb, bias_ext)


GMLP_SUB = 256


def _layer1_kernel(x_ref, ya_ref, yb_ref, woe_ref, g1_ref, wio_ref, lng_ref, lnb_ref, sgw_ref,
                   sgb_ref, woo_ref, fg_ref, o_ref):
    tm = x_ref.shape[0]
    gd = SG_WIDTH // SG_GROUPS
    pr = lax.broadcasted_iota(jnp.int32, (SG_CHUNK, SG_CHUNK), 0) // CHUNK
    pc = lax.broadcasted_iota(jnp.int32, (SG_CHUNK, SG_CHUNK), 1) // CHUNK
    causal = pc <= pr
    wg = [jnp.where(causal, sgw_ref[g], 0.0).astype(BF16) for g in range(SG_GROUPS)]

    subs = [slice(i * GMLP_SUB, (i + 1) * GMLP_SUB) for i in range(tm // GMLP_SUB)]
    h1 = [x_ref[rs, :] + jnp.dot(ya_ref[rs, :], woe_ref[0:RW_WIDTH, :], preferred_element_type=F32)
          + jnp.dot(yb_ref[rs, :], woe_ref[RW_WIDTH:, :], preferred_element_type=F32) for rs in subs]
    n1 = [_rmsnorm(h, g1_ref[...]).astype(BF16) for h in h1]
    vv = [_gelu_tanh(jnp.dot(n, wio_ref[:, SG_WIDTH:2 * SG_WIDTH], preferred_element_type=F32)) for n in n1]
    u = [_gelu_tanh(jnp.dot(n, wio_ref[:, 0:SG_WIDTH], preferred_element_type=F32)) for n in n1]
    gate = [jnp.dot(n, wio_ref[:, 2 * SG_WIDTH:], preferred_element_type=F32) for n in n1]
    vln = []
    for x in vv:
        mu = jnp.mean(x, axis=-1, keepdims=True)
        dv = x - mu
        var = jnp.mean(dv * dv, axis=-1, keepdims=True)
        vln.append((dv * lax.rsqrt(var + LN_EPS) * lng_ref[...] + lnb_ref[...]).astype(BF16))
    sv = []
    for xl in vln:
        cols = [jnp.concatenate([jnp.dot(wg[g], xl[nb * SG_CHUNK:(nb + 1) * SG_CHUNK, g * gd:(g + 1) * gd],
                                         preferred_element_type=F32) + sgb_ref[:, g * gd:(g + 1) * gd]
                                 for nb in range(GMLP_SUB // SG_CHUNK)], axis=0) for g in range(SG_GROUPS)]
        sv.append(jnp.concatenate(cols, axis=1))
    y = [((ui * svi) * _silu(gi)).astype(BF16) for ui, svi, gi in zip(u, sv, gate)]
    for rs, h, yi in zip(subs, h1, y):
        h2 = h + jnp.dot(yi, woo_ref[...], preferred_element_type=F32)
        o_ref[rs, :] = _rmsnorm(h2, fg_ref[...])


def _layer1(x2d, ya, yb, woe, g1, wio, lng, lnb, sgw, sgb_full, woo, fg, tm):
    t = x2d.shape[0]
    row = lambda i: (i, 0)
    const = lambda i: (0, 0)
    vec = pl.BlockSpec((1, D_MODEL), const)
    return pl.pallas_call(
        _layer1_kernel,
        grid=(t // tm,),
        in_specs=[pl.BlockSpec((tm, D_MODEL), row),
                  pl.BlockSpec((tm, RW_WIDTH), row),
                  pl.BlockSpec((tm, AT_WIDTH), row),
                  pl.BlockSpec((D_MODEL, D_MODEL), const),
                  vec,
                  pl.BlockSpec((D_MODEL, 3 * SG_WIDTH), const),
                  vec, vec,
                  pl.BlockSpec((SG_GROUPS, SG_CHUNK, SG_CHUNK), lambda i: (0, 0, 0)),
                  pl.BlockSpec((SG_CHUNK, SG_WIDTH), const),
                  pl.BlockSpec((SG_WIDTH, D_MODEL), const),
                  vec],
        out_specs=pl.BlockSpec((tm, D_MODEL), row),
        out_shape=jax.ShapeDtypeStruct((t, D_MODEL), F32),
        compiler_params=pltpu.CompilerParams(dimension_semantics=("parallel",),
                                             vmem_limit_bytes=VMEM_LIMIT),
        name="gmlp_layer",
    )(x2d, ya, yb, woe, g1, wio, lng, lnb, sgw, sgb_full, woo, fg)


def _bias_table_t(att_bias):
    n = np.arange(-(CHUNK - 1), BAND)
    idx = np.clip(LEFT_CHUNKS * CHUNK - n, -REL_CLIP, REL_CLIP) + REL_CLIP
    diag = att_bias[:, idx]
    bt = jnp.stack([diag[:, CHUNK - 1 - qi:CHUNK - 1 - qi + BAND] for qi in range(CHUNK)], axis=-1)
    bt = jnp.transpose(bt, (1, 0, 2)).reshape(BAND, -1).astype(F32) * LOG2E
    return jnp.concatenate([bt, jnp.full((LEFT_CHUNKS * CHUNK, bt.shape[1]), NEG_INF, F32)], axis=0)


def kernel(x, norm_g, w_in_e, shift_mu, rw_w0, rw_w2, rw_a0, rw_a2, rw_kk, rw_ka, rw_rk, rw_lnx_g,
           rw_lnx_b, att_bias, w_out_e, w_in_o, sg_ln_g, sg_ln_b, sg_w, sg_b, w_out_o, final_g):
    b, s, d = x.shape
    assert d == D_MODEL and s % 512 == 0 and s >= BAND
    x2d = x.reshape(b * s, d)
    row = lambda a: a.reshape(1, -1).astype(F32)

    rw, ga, q, k, v, gb = _in_proj_even(x2d, row(norm_g[0]), w_in_e[0].astype(BF16), row(shift_mu[0]),
                                        seq_len=s, tm=512)
    r3 = lambda a: a.reshape(b, s, a.shape[-1])
    ya = _rwkv_pipelined(r3(rw), r3(ga), row(rw_w0[0]), rw_w2[0], row(rw_a0[0]), rw_a2[0], row(rw_kk[0]),
                         row(rw_ka[0]), row(rw_rk[0]), row(rw_lnx_g[0]), row(rw_lnx_b[0]), nb=2, rb=256)
    yb = _attn(r3(q), r3(k), r3(v), r3(gb), _bias_table_t(att_bias[0]), cb=8)

    sgb_full = jnp.repeat(sg_b[0].T, SG_WIDTH // SG_GROUPS, axis=1).astype(F32)
    out = _layer1(x2d, ya.reshape(b * s, -1), yb.reshape(b * s, -1), w_out_e[0].astype(BF16),
                  row(norm_g[1]), w_in_o[0].astype(BF16), row(sg_ln_g[0]), row(sg_ln_b[0]), sg_w[0],
                  sgb_full, w_out_o[0].astype(BF16), row(final_g), tm=512)
    return out.reshape(b, s, d)
```

```python
import functools
import math

import jax
import jax.numpy as jnp
import numpy as np
from jax import lax
from jax.experimental import pallas as pl
from jax.experimental.pallas import tpu as pltpu

F32 = jnp.float32
BF16 = jnp.bfloat16

D_MODEL = 1024
CHUNK = 64
RW_HEAD_DIM = 64
RW_WIDTH = 512
DECAY_LORA = 64
AAA_LORA = 64
AT_HEAD_DIM = 64
AT_WIDTH = 512
LEFT_CHUNKS = 8
BAND = (LEFT_CHUNKS + 1) * CHUNK
REL_CLIP = 2 * CHUNK
SG_CHUNK = 128
SG_WIDTH = 1024
SG_GROUPS = 8
SHIFT_WIDTH = 3 * RW_WIDTH + DECAY_LORA + AAA_LORA
EVEN_IN = SHIFT_WIDTH + RW_WIDTH + 4 * AT_WIDTH
RMS_EPS = 1e-6
LN_EPS = 1e-5
GN_EPS = 64e-5
NEG_INF = -1e30
LOG2E = math.log2(math.e)

LANES = 128
VMEM_LIMIT = 56 * 1024 * 1024

PAIR = 2 * RW_HEAD_DIM


def _dot(a, b):
    return jnp.dot(a.astype(BF16), b.astype(BF16), preferred_element_type=F32)


def _dot_nt(a, b):
    return lax.dot_general(a.astype(BF16), b.astype(BF16), (((1,), (1,)), ((), ())),
                           preferred_element_type=F32)


def _dot_tn(a, b):
    return lax.dot_general(a.astype(BF16), b.astype(BF16), (((0,), (0,)), ((), ())),
                           preferred_element_type=F32)


def _split(x):
    hi = x.astype(BF16)
    lo = (x - hi.astype(F32)).astype(BF16)
    return hi, lo


def _dot_exact_lhs(a_bf16, x):
    hi, lo = _split(x)
    return _dot(a_bf16, hi) + _dot(a_bf16, lo)


def _dot_exact_rhs(x, b_bf16):
    hi, lo = _split(x)
    return _dot(hi, b_bf16) + _dot(lo, b_bf16)


def _sigmoid(x):
    return 1.0 / (1.0 + jnp.exp(-x))


def _silu(x):
    return x * _sigmoid(x)


def _gelu_tanh(x):
    c = math.sqrt(2.0 / math.pi)
    return 0.5 * x * (1.0 + jnp.tanh(c * (x + 0.044715 * (x * x * x))))


def _rmsnorm(x, g):
    ms = jnp.mean(x * x, axis=-1, keepdims=True)
    return x * lax.rsqrt(ms + RMS_EPS) * g


def _col_chunks(lo, hi, width=256):
    out = []
    c = lo
    while c < hi:
        w = min(width, hi - c)
        out.append((c, w))
        c += w
    return out


def _in_proj_even_kernel(x_ref, g_ref, w_ref, mu_ref, rw_ref, ga_ref, q_ref, k_ref, v_ref, gb_ref,
                         n_ref, carry_ref, *, tiles_per_seq):
    tm = x_ref.shape[0]
    i = pl.program_id(0)
    n_ref[...] = _rmsnorm(x_ref[...], g_ref[...]).astype(BF16)

    @pl.when(i % tiles_per_seq == 0)
    def _():
        carry_ref[...] = jnp.zeros_like(carry_ref)

    for c0, cw in _col_chunks(0, SHIFT_WIDTH):
        pc = jnp.dot(n_ref[...], w_ref[:, c0:c0 + cw], preferred_element_type=F32)
        row0 = lax.broadcasted_iota(jnp.int32, (tm, cw), 0) == 0
        prev = jnp.where(row0, carry_ref[0:1, c0:c0 + cw], pltpu.roll(pc, 1, 0))
        carry_ref[0:1, c0:c0 + cw] = pc[tm - 1:tm, :]
        rw_ref[:, c0:c0 + cw] = pc + (prev - pc) * mu_ref[:, c0:c0 + cw]

    base = SHIFT_WIDTH
    for dst, scale in ((ga_ref, None), (q_ref, LOG2E / math.sqrt(AT_HEAD_DIM)), (k_ref, None),
                       (v_ref, None), (gb_ref, None)):
        for c0, cw in _col_chunks(0, RW_WIDTH):
            pc = jnp.dot(n_ref[...], w_ref[:, base + c0:base + c0 + cw], preferred_element_type=F32)
            if scale is not None:
                pc = pc * scale
            dst[:, c0:c0 + cw] = pc.astype(dst.dtype)
        base += RW_WIDTH


def _in_proj_even(x2d, g, w_bf16, mu, seq_len, tm):
    t = x2d.shape[0]
    row = lambda i: (i, 0)
    const = lambda i: (0, 0)
    kern = functools.partial(_in_proj_even_kernel, tiles_per_seq=seq_len // tm)
    return pl.pallas_call(
        kern,
        grid=(t // tm,),
        in_specs=[pl.BlockSpec((tm, D_MODEL), row),
                  pl.BlockSpec((1, D_MODEL), const),
                  pl.BlockSpec((D_MODEL, EVEN_IN), const),
                  pl.BlockSpec((1, SHIFT_WIDTH), const)],
        out_specs=[pl.BlockSpec((tm, SHIFT_WIDTH), row),
                   pl.BlockSpec((tm, RW_WIDTH), row),
                   pl.BlockSpec((tm, AT_WIDTH), row),
                   pl.BlockSpec((tm, AT_WIDTH), row),
                   pl.BlockSpec((tm, AT_WIDTH), row),
                   pl.BlockSpec((tm, AT_WIDTH), row)],
        out_shape=[jax.ShapeDtypeStruct((t, SHIFT_WIDTH), F32),
                   jax.ShapeDtypeStruct((t, RW_WIDTH), F32),
                   jax.ShapeDtypeStruct((t, AT_WIDTH), BF16),
                   jax.ShapeDtypeStruct((t, AT_WIDTH), BF16),
                   jax.ShapeDtypeStruct((t, AT_WIDTH), BF16),
                   jax.ShapeDtypeStruct((t, AT_WIDTH), F32)],
        scratch_shapes=[pltpu.VMEM((tm, D_MODEL), BF16),
                        pltpu.VMEM((8, SHIFT_WIDTH), F32)],
        compiler_params=pltpu.CompilerParams(dimension_semantics=("arbitrary",),
                                             vmem_limit_bytes=VMEM_LIMIT),
        name="in_proj_even",
    )(x2d, g, w_bf16, mu)


EXP_NEG_HALF = math.exp(-0.5)


def _rwkv_kernel(rw_ref, ga_ref, w0_ref, w2_ref, a0_ref, a2_ref, kk_ref, ka_ref, rk_ref, lg_ref,
                 lb_ref, o_ref, z_ref):
    rb = rw_ref.shape[0]
    n_chunks = rb // CHUNK
    n_pairs = RW_WIDTH // PAIR

    @pl.when(pl.program_id(1) == 0)
    def _():
        z_ref[...] = jnp.zeros_like(z_ref)

    ri = lax.broadcasted_iota(jnp.int32, (PAIR, PAIR), 0)
    ci = lax.broadcasted_iota(jnp.int32, (PAIR, PAIR), 1)
    same_head = (ri // RW_HEAD_DIM) == (ci // RW_HEAD_DIM)
    tril_bd = jnp.logical_and(same_head, ci <= ri)
    stril_bd = jnp.logical_and(same_head, ci < ri)
    eye_mask = ri == ci
    eye = jnp.where(eye_mask, 1.0, 0.0).astype(F32)
    blk = {sz: (ri // sz) == (ci // sz) for sz in (8, 16, 32, 64)}
    merge = {sz: jnp.logical_and(blk[2 * sz], jnp.logical_not(blk[sz])) for sz in (8, 16, 32)}
    ones_bd = jnp.where(same_head, 1.0, 0.0).astype(BF16)
    tr = lax.broadcasted_iota(jnp.int32, (rb, rb), 0)
    tc = lax.broadcasted_iota(jnp.int32, (rb, rb), 1)
    same_chunk = (tr // CHUNK) == (tc // CHUNK)
    chunk_tril = jnp.where(jnp.logical_and(same_chunk, tc <= tr), 1.0, 0.0).astype(BF16)
    lane_head = lax.broadcasted_iota(jnp.int32, (CHUNK, PAIR), 1) // RW_HEAD_DIM
    head0 = lane_head == 0
    head1 = lane_head == 1

    first_head = lax.broadcasted_iota(jnp.int32, (rb, PAIR), 1) < RW_HEAD_DIM

    def head_sum(x):
        out = []
        for p in range(n_pairs):
            xs = x[:, p * PAIR:(p + 1) * PAIR]
            s0 = jnp.sum(jnp.where(first_head, xs, 0.0), axis=1, keepdims=True)
            s1 = jnp.sum(jnp.where(first_head, 0.0, xs), axis=1, keepdims=True)
            out.append(jnp.where(first_head, s0, s1))
        return jnp.concatenate(out, axis=1)

    def bd(x, cc, p):
        xs = x[cc * CHUNK:(cc + 1) * CHUNK, p * PAIR:(p + 1) * PAIR]
        return jnp.concatenate([jnp.where(head0, xs, 0.0), jnp.where(head1, xs, 0.0)], axis=0)

    r = rw_ref[:, 0:RW_WIDTH]
    k = rw_ref[:, RW_WIDTH:2 * RW_WIDTH]
    v = rw_ref[:, 2 * RW_WIDTH:3 * RW_WIDTH]
    wd = rw_ref[:, 3 * RW_WIDTH:3 * RW_WIDTH + DECAY_LORA]
    ad = rw_ref[:, 3 * RW_WIDTH + DECAY_LORA:SHIFT_WIDTH]

    zw = w0_ref[...] + _dot(jnp.tanh(wd), w2_ref[...])
    lw = -EXP_NEG_HALF * _sigmoid(zw)
    lw_hi, lw_lo = _split(lw)
    c = _dot(chunk_tril, lw_hi) + _dot(chunk_tril, lw_lo)
    c_last = jnp.concatenate(
        [jnp.broadcast_to(c[(cc + 1) * CHUNK - 1:(cc + 1) * CHUNK, :], (CHUNK, RW_WIDTH))
         for cc in range(n_chunks)], axis=0)
    e_c = jnp.exp(c)
    e_nc = jnp.exp(-c)
    e_cm = jnp.exp(c - lw)
    e_lc = jnp.exp(c_last - c)
    g_l = jnp.exp(c_last)

    a_g = _sigmoid(a0_ref[...] + _dot(ad, a2_ref[...]))
    kk = k * kk_ref[...]
    ss = head_sum(kk * kk)
    kkn = kk * lax.rsqrt(jnp.maximum(ss, 1e-24))
    k_m = k * (1.0 + (a_g - 1.0) * ka_ref[...])
    b_v = kkn * a_g
    rh = r * e_c
    kh = k_m * e_nc
    bh = b_v * e_nc
    ah = -kkn * e_cm
    bt = b_v * e_lc
    kt = k_m * e_lc
    bonus = head_sum(r * k_m * rk_ref[...]) * v

    chains = [(cc, p) for cc in range(n_chunks) for p in range(n_pairs)]
    g_mat, rh_bd, ah_bd, v_bd, bt_t, kt_t, g_last = {}, {}, {}, {}, {}, {}, {}
    for ch in chains:
        cc, p = ch
        rows = slice(cc * CHUNK, (cc + 1) * CHUNK)
        sl = slice(p * PAIR, (p + 1) * PAIR)
        rh_bd[ch] = bd(rh, cc, p)
        ah_bd[ch] = bd(ah, cc, p)
        v_bd[ch] = bd(v, cc, p)
        bt_t[ch] = bd(bt, cc, p).T
        kt_t[ch] = bd(kt, cc, p).T
        g_last[ch] = g_l[cc * CHUNK:cc * CHUNK + 1, sl]
        lhs = jnp.concatenate([rh_bd[ch], ah_bd[ch]], axis=0)
        rhs = jnp.concatenate([kh[rows, sl], bh[rows, sl]], axis=0)
        g_mat[ch] = _dot_nt(lhs, rhs)

    g_swap = {ch: pltpu.roll(g_mat[ch], RW_HEAD_DIM, 1) for ch in chains}
    H = RW_HEAD_DIM

    def blocks(top, bottom, mask):
        return jnp.where(mask, jnp.concatenate([top, bottom], axis=0), 0.0)

    a_rk = {ch: blocks(g_mat[ch][0:H], g_swap[ch][H:2 * H], tril_bd) for ch in chains}
    a_rb = {ch: blocks(g_swap[ch][0:H], g_mat[ch][H:2 * H], tril_bd) for ch in chains}
    a_ak = {ch: blocks(g_mat[ch][2 * H:3 * H], g_swap[ch][3 * H:4 * H], stril_bd) for ch in chains}
    a_ab = {ch: blocks(g_swap[ch][2 * H:3 * H], g_mat[ch][3 * H:4 * H], stril_bd) for ch in chains}

    a8 = {ch: jnp.where(blk[8], a_ab[ch], 0.0) for ch in chains}
    a2 = {ch: _dot(a8[ch], a8[ch]) for ch in chains}
    b1 = {ch: eye + a8[ch] for ch in chains}
    b2 = {ch: b1[ch] + _dot(a2[ch], b1[ch]) for ch in chains}
    a4 = {ch: _dot(a2[ch], a2[ch]) for ch in chains}
    t_inv = {ch: b2[ch] + _dot(a4[ch], b2[ch]) for ch in chains}
    def low_rows(x, sz):
        return jnp.concatenate([x[s:s + sz] for s in range(sz, PAIR, 2 * sz)], axis=0)

    def spread_low_rows(y, sz):
        zero = jnp.zeros((sz, y.shape[1]), y.dtype)
        parts = []
        for i in range(PAIR // (2 * sz)):
            parts += [zero, y[i * sz:(i + 1) * sz]]
        return jnp.concatenate(parts, axis=0)

    for sz in (8, 16, 32):
        ed = {ch: _dot(low_rows(jnp.where(merge[sz], a_ab[ch], 0.0), sz), t_inv[ch]) for ch in chains}
        upd = {ch: _dot(low_rows(t_inv[ch], sz), spread_low_rows(ed[ch], sz)) for ch in chains}
        t_inv = {ch: t_inv[ch] + spread_low_rows(upd[ch], sz) for ch in chains}

    x1 = {ch: _dot(a_ak[ch], v_bd[ch]) for ch in chains}
    wu = {ch: _dot(t_inv[ch], jnp.concatenate([ah_bd[ch], x1[ch]], axis=1)) for ch in chains}
    zeros = jnp.zeros((PAIR, PAIR), F32)
    qymn = {}
    for ch in chains:
        lhs = jnp.concatenate([jnp.concatenate([a_rb[ch], a_rk[ch]], axis=1),
                               jnp.concatenate([bt_t[ch], kt_t[ch]], axis=1)], axis=0)
        rhs = jnp.concatenate([wu[ch], jnp.concatenate([zeros, v_bd[ch]], axis=1)], axis=0)
        qymn[ch] = _dot(lhs, rhs)

    z = [z_ref[p] for p in range(n_pairs)]
    y_rows = []
    for cc in range(n_chunks):
        y_pairs = []
        for p in range(n_pairs):
            ch = (cc, p)
            q = rh_bd[ch] + qymn[ch][0:PAIR, 0:PAIR]
            m = qymn[ch][PAIR:2 * PAIR, 0:PAIR] + jnp.where(eye_mask, g_last[ch], 0.0)
            zy = _dot(jnp.concatenate([m, q], axis=0), z[p])
            z[p] = zy[0:PAIR] + qymn[ch][PAIR:2 * PAIR, PAIR:2 * PAIR]
            y_bd = zy[PAIR:2 * PAIR] + qymn[ch][0:PAIR, PAIR:2 * PAIR]
            y_pairs.append(y_bd[0:CHUNK] + y_bd[CHUNK:PAIR])
        y_rows.append(jnp.concatenate(y_pairs, axis=1))
    for p in range(n_pairs):
        z_ref[p] = z[p]
    y = jnp.concatenate(y_rows, axis=0)
    mu = head_sum(y) * (1.0 / RW_HEAD_DIM)
    d = y - mu
    var = head_sum(d * d) * (1.0 / RW_HEAD_DIM)
    yn = d * lax.rsqrt(var + GN_EPS) * lg_ref[...] + lb_ref[...] + bonus
    o_ref[...] = (yn * _silu(ga_ref[...])).astype(o_ref.dtype)


def _rwkv(rw, ga, w0, w2, a0, a2, k_k, k_a, r_k, lnx_g, lnx_b, rb):
    b, s, _ = rw.shape
    blk3 = lambda i, j: (i, j, 0)
    const = lambda i, j: (0, 0)
    vec = pl.BlockSpec((1, RW_WIDTH), const)
    lora = pl.BlockSpec((DECAY_LORA, RW_WIDTH), const)
    return pl.pallas_call(
        _rwkv_kernel,
        grid=(b, s // rb),
        in_specs=[pl.BlockSpec((None, rb, SHIFT_WIDTH), blk3),
                  pl.BlockSpec((None, rb, RW_WIDTH), blk3),
                  vec, lora, vec, lora, vec, vec, vec, vec, vec],
        out_specs=pl.BlockSpec((None, rb, RW_WIDTH), blk3),
        out_shape=jax.ShapeDtypeStruct((b, s, RW_WIDTH), BF16),
        scratch_shapes=[pltpu.VMEM((RW_WIDTH // PAIR, PAIR, PAIR), F32)],
        compiler_params=pltpu.CompilerParams(dimension_semantics=("parallel", "arbitrary"),
                                             vmem_limit_bytes=VMEM_LIMIT),
        name="rwkv7_mix",
    )(rw, ga, w0, w2, a0, a2, k_k, k_a, r_k, lnx_g, lnx_b)


def _rwkv_pipelined_kernel(rw_ref, ga_ref, w0_ref, w2_ref, a0_ref, a2_ref, kk_ref, ka_ref, rk_ref,
                           lg_ref, lb_ref, o_ref, z_ref):
    nb, rb = rw_ref.shape[0], rw_ref.shape[1]
    n_chunks = rb // CHUNK
    n_pairs = RW_WIDTH // PAIR
    H = RW_HEAD_DIM

    @pl.when(pl.program_id(1) == 0)
    def _():
        z_ref[...] = jnp.zeros_like(z_ref)

    ri = lax.broadcasted_iota(jnp.int32, (PAIR, PAIR), 0)
    ci = lax.broadcasted_iota(jnp.int32, (PAIR, PAIR), 1)
    same_head = (ri // H) == (ci // H)
    tril_bd = jnp.logical_and(same_head, ci <= ri)
    stril_bd = jnp.logical_and(same_head, ci < ri)
    eye_mask = ri == ci
    eye = jnp.where(eye_mask, 1.0, 0.0).astype(F32)
    blk = {sz: (ri // sz) == (ci // sz) for sz in (8, 16, 32, 64)}
    merge = {sz: jnp.logical_and(blk[2 * sz], jnp.logical_not(blk[sz])) for sz in (8, 16, 32)}
    tr = lax.broadcasted_iota(jnp.int32, (rb, rb), 0)
    tc = lax.broadcasted_iota(jnp.int32, (rb, rb), 1)
    chunk_tril = jnp.where(jnp.logical_and((tr // CHUNK) == (tc // CHUNK), tc <= tr), 1.0, 0.0).astype(BF16)
    lane_head = lax.broadcasted_iota(jnp.int32, (CHUNK, PAIR), 1) // H
    head0 = lane_head == 0
    head1 = lane_head == 1
    first_head = lax.broadcasted_iota(jnp.int32, (rb, PAIR), 1) < H
    zeros = jnp.zeros((PAIR, PAIR), F32)

    def head_sum(x):
        out = []
        for p in range(n_pairs):
            xs = x[:, p * PAIR:(p + 1) * PAIR]
            s0 = jnp.sum(jnp.where(first_head, xs, 0.0), axis=1, keepdims=True)
            s1 = jnp.sum(jnp.where(first_head, 0.0, xs), axis=1, keepdims=True)
            out.append(jnp.where(first_head, s0, s1))
        return jnp.concatenate(out, axis=1)

    def bd(x, cc, p):
        xs = x[cc * CHUNK:(cc + 1) * CHUNK, p * PAIR:(p + 1) * PAIR]
        return jnp.concatenate([jnp.where(head0, xs, 0.0), jnp.where(head1, xs, 0.0)], axis=0)

    def low_rows(x, sz):
        return jnp.concatenate([x[s:s + sz] for s in range(sz, PAIR, 2 * sz)], axis=0)

    def spread_low_rows(y, sz):
        zero = jnp.zeros((sz, y.shape[1]), y.dtype)
        parts = []
        for i in range(PAIR // (2 * sz)):
            parts += [zero, y[i * sz:(i + 1) * sz]]
        return jnp.concatenate(parts, axis=0)

    def blocks(top, bottom, mask):
        return jnp.where(mask, jnp.concatenate([top, bottom], axis=0), 0.0)

    chains = [(cc, p) for cc in range(n_chunks) for p in range(n_pairs)]
    st = [dict() for _ in range(nb)]

    def prep_lora(bi):
        s = st[bi]
        wd = rw_ref[bi, :, 3 * RW_WIDTH:3 * RW_WIDTH + DECAY_LORA]
        ad = rw_ref[bi, :, 3 * RW_WIDTH + DECAY_LORA:SHIFT_WIDTH]
        s["zw"] = w0_ref[...] + _dot(jnp.tanh(wd), w2_ref[...])
        s["za"] = a0_ref[...] + _dot(ad, a2_ref[...])

    def prep_decay(bi):
        s = st[bi]
        lw = -EXP_NEG_HALF * _sigmoid(s["zw"])
        lw_hi, lw_lo = _split(lw)
        s["lw"] = lw
        s["c"] = _dot(chunk_tril, lw_hi) + _dot(chunk_tril, lw_lo)

    def prep_rows(bi):
        s = st[bi]
        r = rw_ref[bi, :, 0:RW_WIDTH]
        k = rw_ref[bi, :, RW_WIDTH:2 * RW_WIDTH]
        v = rw_ref[bi, :, 2 * RW_WIDTH:3 * RW_WIDTH]
        c, lw = s["c"], s["lw"]
        c_last = jnp.concatenate(
            [jnp.broadcast_to(c[(cc + 1) * CHUNK - 1:(cc + 1) * CHUNK, :], (CHUNK, RW_WIDTH))
             for cc in range(n_chunks)], axis=0)
        e_c = jnp.exp(c)
        e_nc = jnp.exp(-c)
        e_cm = jnp.exp(c - lw)
        e_lc = jnp.exp(c_last - c)
        g_l = jnp.exp(c_last)
        a_g = _sigmoid(s["za"])
        kk = k * kk_ref[...]
        ss = head_sum(kk * kk)
        kkn = kk * lax.rsqrt(jnp.maximum(ss, 1e-24))
        k_m = k * (1.0 + (a_g - 1.0) * ka_ref[...])
        b_v = kkn * a_g
        rh = r * e_c
        kh = k_m * e_nc
        bh = b_v * e_nc
        ah = -kkn * e_cm
        bt = b_v * e_lc
        kt = k_m * e_lc
        s["bonus"] = head_sum(r * k_m * rk_ref[...]) * v
        for key in ("rh_bd", "ah_bd", "v_bd", "bt_t", "kt_t", "g_last", "g"):
            s[key] = {}
        for ch in chains:
            cc, p = ch
            rows = slice(cc * CHUNK, (cc + 1) * CHUNK)
            sl = slice(p * PAIR, (p + 1) * PAIR)
            s["rh_bd"][ch] = bd(rh, cc, p)
            s["ah_bd"][ch] = bd(ah, cc, p)
            s["v_bd"][ch] = bd(v, cc, p)
            s["bt_t"][ch] = bd(bt, cc, p).T
            s["kt_t"][ch] = bd(kt, cc, p).T
            s["g_last"][ch] = g_l[cc * CHUNK:cc * CHUNK + 1, sl]
            lhs = jnp.concatenate([s["rh_bd"][ch], s["ah_bd"][ch]], axis=0)
            rhs = jnp.concatenate([kh[rows, sl], bh[rows, sl]], axis=0)
            s["g"][ch] = _dot_nt(lhs, rhs)

    def m_blocks(bi):
        s = st[bi]
        g = s.pop("g")
        gs = {ch: pltpu.roll(g[ch], H, 1) for ch in chains}
        s["a_rk"] = {ch: blocks(g[ch][0:H], gs[ch][H:2 * H], tril_bd) for ch in chains}
        s["a_rb"] = {ch: blocks(gs[ch][0:H], g[ch][H:2 * H], tril_bd) for ch in chains}
        s["a_ak"] = {ch: blocks(g[ch][2 * H:3 * H], gs[ch][3 * H:4 * H], stril_bd) for ch in chains}
        s["a_ab"] = {ch: blocks(gs[ch][2 * H:3 * H], g[ch][3 * H:4 * H], stril_bd) for ch in chains}
        s["a8"] = {ch: jnp.where(blk[8], s["a_ab"][ch], 0.0) for ch in chains}
        s["a2"] = {ch: _dot(s["a8"][ch], s["a8"][ch]) for ch in chains}

    def m_neumann2(bi):
        s = st[bi]
        b1 = {ch: eye + s["a8"][ch] for ch in chains}
        s["b2"] = {ch: b1[ch] + _dot(s["a2"][ch], b1[ch]) for ch in chains}
        s["a4"] = {ch: _dot(s["a2"][ch], s["a2"][ch]) for ch in chains}

    def m_neumann3(bi):
        s = st[bi]
        s["t"] = {ch: s["b2"][ch] + _dot(s["a4"][ch], s["b2"][ch]) for ch in chains}

    def m_merge_a(sz):
        def f(bi):
            s = st[bi]
            s["ed"] = {ch: _dot(low_rows(jnp.where(merge[sz], s["a_ab"][ch], 0.0), sz), s["t"][ch])
                       for ch in chains}
        return f

    def m_merge_b(sz):
        def f(bi):
            s = st[bi]
            upd = {ch: _dot(low_rows(s["t"][ch], sz), spread_low_rows(s["ed"][ch], sz)) for ch in chains}
            s["t"] = {ch: s["t"][ch] + spread_low_rows(upd[ch], sz) for ch in chains}
        return f

    def m_x1(bi):
        s = st[bi]
        s["x1"] = {ch: _dot(s["a_ak"][ch], s["v_bd"][ch]) for ch in chains}

    def m_wu(bi):
        s = st[bi]
        s["wu"] = {ch: _dot(s["t"][ch], jnp.concatenate([s["ah_bd"][ch], s["x1"][ch]], axis=1))
                   for ch in chains}

    def m_qymn(bi):
        s = st[bi]
        s["qymn"] = {}
        for ch in chains:
            lhs = jnp.concatenate([jnp.concatenate([s["a_rb"][ch], s["a_rk"][ch]], axis=1),
                                   jnp.concatenate([s["bt_t"][ch], s["kt_t"][ch]], axis=1)], axis=0)
            rhs = jnp.concatenate([s["wu"][ch], jnp.concatenate([zeros, s["v_bd"][ch]], axis=1)], axis=0)
            s["qymn"][ch] = _dot(lhs, rhs)
        s["z"] = [z_ref[bi, p] for p in range(n_pairs)]
        s["y_rows"] = []

    def tail(cc):
        def f(bi):
            s = st[bi]
            y_pairs = []
            for p in range(n_pairs):
                ch = (cc, p)
                qymn = s["qymn"][ch]
                q = s["rh_bd"][ch] + qymn[0:PAIR, 0:PAIR]
                m = qymn[PAIR:2 * PAIR, 0:PAIR] + jnp.where(eye_mask, s["g_last"][ch], 0.0)
                zy = _dot(jnp.concatenate([m, q], axis=0), s["z"][p])
                s["z"][p] = zy[0:PAIR] + qymn[PAIR:2 * PAIR, PAIR:2 * PAIR]
                y_bd = zy[PAIR:2 * PAIR] + qymn[0:PAIR, PAIR:2 * PAIR]
                y_pairs.append(y_bd[0:CHUNK] + y_bd[CHUNK:PAIR])
            s["y_rows"].append(jnp.concatenate(y_pairs, axis=1))
        return f

    def epilogue(bi):
        s = st[bi]
        for p in range(n_pairs):
            z_ref[bi, p] = s["z"][p]
        y = jnp.concatenate(s["y_rows"], axis=0)
        mu = head_sum(y) * (1.0 / H)
        d = y - mu
        var = head_sum(d * d) * (1.0 / H)
        yn = d * lax.rsqrt(var + GN_EPS) * lg_ref[...] + lb_ref[...] + s["bonus"]
        o_ref[bi] = (yn * _silu(ga_ref[bi])).astype(o_ref.dtype)
        s.clear()

    prep = [prep_lora, prep_decay, prep_rows]
    middle = [m_blocks, m_neumann2, m_neumann3]
    for sz in (8, 16, 32):
        middle += [m_merge_a(sz), m_merge_b(sz)]
    middle += [m_x1, m_wu, m_qymn]
    finish = [tail(cc) for cc in range(n_chunks)] + [epilogue]

    for f in prep:
        f(0)
    for bi in range(nb):
        side = []
        if bi > 0:
            side += [(g, bi - 1) for g in finish]
        if bi + 1 < nb:
            side += [(g, bi + 1) for g in prep]
        every = max(1, len(middle) // (len(side) + 1)) if side else 0
        for i, f in enumerate(middle):
            f(bi)
            if side and (i + 1) % every == 0:
                g, other = side.pop(0)
                g(other)
        for g, other in side:
            g(other)
    for g in finish:
        g(nb - 1)


def _rwkv_pipelined(rw, ga, w0, w2, a0, a2, k_k, k_a, r_k, lnx_g, lnx_b, nb, rb):
    b, s, _ = rw.shape
    blk3 = lambda i, j: (i, j, 0)
    const = lambda i, j: (0, 0)
    vec = pl.BlockSpec((1, RW_WIDTH), const)
    lora = pl.BlockSpec((DECAY_LORA, RW_WIDTH), const)
    return pl.pallas_call(
        _rwkv_pipelined_kernel,
        grid=(b // nb, s // rb),
        in_specs=[pl.BlockSpec((nb, rb, SHIFT_WIDTH), blk3),
                  pl.BlockSpec((nb, rb, RW_WIDTH), blk3),
                  vec, lora, vec, lora, vec, vec, vec, vec, vec],
        out_specs=pl.BlockSpec((nb, rb, RW_WIDTH), blk3),
        out_shape=jax.ShapeDtypeStruct((b, s, RW_WIDTH), BF16),
        scratch_shapes=[pltpu.VMEM((nb, RW_WIDTH // PAIR, PAIR, PAIR), F32)],
        compiler_params=pltpu.CompilerParams(dimension_semantics=("parallel", "arbitrary"),
                                             vmem_limit_bytes=VMEM_LIMIT),
        name="rwkv7_mix",
    )(rw, ga, w0, w2, a0, a2, k_k, k_a, r_k, lnx_g, lnx_b)


SLAB = 4 * AT_HEAD_DIM
KEY_WIN = (LEFT_CHUNKS + 2) * CHUNK
N_BIAS = LEFT_CHUNKS + 2


def _attn_kernel(q_ref, k_ref, v_ref, gb_ref, bias_ref, o_ref):
    cb = q_ref.shape[0] // CHUNK
    j = pl.program_id(1)
    lane_head = lax.broadcasted_iota(jnp.int32, (CHUNK, SLAB), 1) // AT_HEAD_DIM
    head_masks = [lane_head == h for h in range(SLAB // AT_HEAD_DIM)]
    for cc in range(cb):
        c = j * cb + cc
        start = pl.multiple_of(jnp.maximum(c - (LEFT_CHUNKS + 1), 0) * CHUNK, CHUNK)
        table = jnp.minimum(c, N_BIAS - 1)
        rows = slice(cc * CHUNK, (cc + 1) * CHUNK)
        for s0 in range(0, AT_WIDTH, SLAB):
            cols = slice(s0, s0 + SLAB)
            qs = q_ref[rows, cols]
            q_bd = jnp.concatenate([jnp.where(mh, qs, jnp.zeros_like(qs)) for mh in head_masks], axis=0)
            kb = k_ref[pl.ds(start, KEY_WIN), cols]
            s = lax.dot_general(q_bd, kb, (((1,), (1,)), ((), ())), preferred_element_type=F32)
            s = s + bias_ref[table, s0:s0 + SLAB, :]
            mx = jnp.max(s, axis=1, keepdims=True)
            e = jnp.exp2(s - mx)
            l = jnp.sum(e, axis=1, keepdims=True)
            vb = v_ref[pl.ds(start, KEY_WIN), cols]
            o_bd = jnp.dot(e.astype(BF16), vb, preferred_element_type=F32) * (1.0 / l)
            o = jnp.zeros((CHUNK, SLAB), F32)
            for h, mh in enumerate(head_masks):
                o = jnp.where(mh, o_bd[h * CHUNK:(h + 1) * CHUNK, :], o)
            o_ref[rows, cols] = (o * _silu(gb_ref[rows, cols])).astype(o_ref.dtype)


def _attn(q, k, v, gb, bias_tables, cb):
    b, s, _ = q.shape
    qblk = lambda i, j: (i, j, 0)
    full = lambda i, j: (i, 0, 0)
    return pl.pallas_call(
        _attn_kernel,
        grid=(b, s // (cb * CHUNK)),
        in_specs=[pl.BlockSpec((None, cb * CHUNK, AT_WIDTH), qblk),
                  pl.BlockSpec((None, s, AT_WIDTH), full),
                  pl.BlockSpec((None, s, AT_WIDTH), full),
                  pl.BlockSpec((None, cb * CHUNK, AT_WIDTH), qblk),
                  pl.BlockSpec(bias_tables.shape, lambda i, j: (0, 0, 0))],
        out_specs=pl.BlockSpec((None, cb * CHUNK, AT_WIDTH), qblk),
        out_shape=jax.ShapeDtypeStruct((b, s, AT_WIDTH), BF16),
        compiler_params=pltpu.CompilerParams(dimension_semantics=("parallel", "arbitrary"),
                                             vmem_limit_bytes=VMEM_LIMIT),
        name="chunk_attention",
    )(q, k, v, gb, bias_tables)


GMLP_SUB = 256


def _layer1_kernel(x_ref, ya_ref, yb_ref, woe_ref, g1_ref, wio_ref, lng_ref, lnb_ref, sgw_ref,
                   sgb_ref, woo_ref, fg_ref, o_ref):
    tm = x_ref.shape[0]
    gd = SG_WIDTH // SG_GROUPS
    pr = lax.broadcasted_iota(jnp.int32, (SG_CHUNK, SG_CHUNK), 0) // CHUNK
    pc = lax.broadcasted_iota(jnp.int32, (SG_CHUNK, SG_CHUNK), 1) // CHUNK
    causal = pc <= pr
    wg = [jnp.where(causal, sgw_ref[g], 0.0).astype(BF16) for g in range(SG_GROUPS)]

    subs = [slice(i * GMLP_SUB, (i + 1) * GMLP_SUB) for i in range(tm // GMLP_SUB)]
    h1 = [x_ref[rs, :] + jnp.dot(ya_ref[rs, :], woe_ref[0:RW_WIDTH, :], preferred_element_type=F32)
          + jnp.dot(yb_ref[rs, :], woe_ref[RW_WIDTH:, :], preferred_element_type=F32) for rs in subs]
    n1 = [_rmsnorm(h, g1_ref[...]).astype(BF16) for h in h1]
    vv = [_gelu_tanh(jnp.dot(n, wio_ref[:, SG_WIDTH:2 * SG_WIDTH], preferred_element_type=F32)) for n in n1]
    u = [_gelu_tanh(jnp.dot(n, wio_ref[:, 0:SG_WIDTH], preferred_element_type=F32)) for n in n1]
    gate = [jnp.dot(n, wio_ref[:, 2 * SG_WIDTH:], preferred_element_type=F32) for n in n1]
    vln = []
    for x in vv:
        mu = jnp.mean(x, axis=-1, keepdims=True)
        dv = x - mu
        var = jnp.mean(dv * dv, axis=-1, keepdims=True)
        vln.append((dv * lax.rsqrt(var + LN_EPS) * lng_ref[...] + lnb_ref[...]).astype(BF16))
    sv = []
    for xl in vln:
        cols = [jnp.concatenate([jnp.dot(wg[g], xl[nb * SG_CHUNK:(nb + 1) * SG_CHUNK, g * gd:(g + 1) * gd],
                                         preferred_element_type=F32) + sgb_ref[:, g * gd:(g + 1) * gd]
                                 for nb in range(GMLP_SUB // SG_CHUNK)], axis=0) for g in range(SG_GROUPS)]
        sv.append(jnp.concatenate(cols, axis=1))
    y = [((ui * svi) * _silu(gi)).astype(BF16) for ui, svi, gi in zip(u, sv, gate)]
    for rs, h, yi in zip(subs, h1, y):
        h2 = h + jnp.dot(yi, woo_ref[...], preferred_element_type=F32)
        o_ref[rs, :] = _rmsnorm(h2, fg_ref[...])


def _layer1(x2d, ya, yb, woe, g1, wio, lng, lnb, sgw, sgb_full, woo, fg, tm):
    t = x2d.shape[0]
    row = lambda i: (i, 0)
    const = lambda i: (0, 0)
    vec = pl.BlockSpec((1, D_MODEL), const)
    return pl.pallas_call(
        _layer1_kernel,
        grid=(t // tm,),
        in_specs=[pl.BlockSpec((tm, D_MODEL), row),
                  pl.BlockSpec((tm, RW_WIDTH), row),
                  pl.BlockSpec((tm, AT_WIDTH), row),
                  pl.BlockSpec((D_MODEL, D_MODEL), const),
                  vec,
                  pl.BlockSpec((D_MODEL, 3 * SG_WIDTH), const),
                  vec, vec,
                  pl.BlockSpec((SG_GROUPS, SG_CHUNK, SG_CHUNK), lambda i: (0, 0, 0)),
                  pl.BlockSpec((SG_CHUNK, SG_WIDTH), const),
                  pl.BlockSpec((SG_WIDTH, D_MODEL), const),
                  vec],
        out_specs=pl.BlockSpec((tm, D_MODEL), row),
        out_shape=jax.ShapeDtypeStruct((t, D_MODEL), F32),
        compiler_params=pltpu.CompilerParams(dimension_semantics=("parallel",),
                                             vmem_limit_bytes=VMEM_LIMIT),
        name="gmlp_layer",
    )(x2d, ya, yb, woe, g1, wio, lng, lnb, sgw, sgb_full, woo, fg)


def _bias_tables(att_bias):
    shift = np.array([c * CHUNK for c in range(N_BIAS - 1)] + [(LEFT_CHUNKS + 1) * CHUNK])
    n = np.arange(-(CHUNK - 1), KEY_WIN)
    idx = np.clip(shift[:, None] - n[None, :], -REL_CLIP, REL_CLIP) + REL_CLIP
    diag = jnp.transpose(att_bias[:, idx], (1, 0, 2))
    bt = jnp.stack([diag[:, :, CHUNK - 1 - qi:CHUNK - 1 - qi + KEY_WIN] for qi in range(CHUNK)], axis=2)
    kj = np.arange(KEY_WIN)
    lo = np.array([0] * (N_BIAS - 1) + [CHUNK])
    hi = np.array([(c + 1) * CHUNK for c in range(N_BIAS - 1)] + [KEY_WIN])
    valid = (kj[None, :] >= lo[:, None]) & (kj[None, :] < hi[:, None])
    bt = jnp.where(valid[:, None, None, :], bt.astype(F32) * LOG2E, NEG_INF)
    return bt.reshape(N_BIAS, -1, KEY_WIN)


def kernel(x, norm_g, w_in_e, shift_mu, rw_w0, rw_w2, rw_a0, rw_a2, rw_kk, rw_ka, rw_rk, rw_lnx_g,
           rw_lnx_b, att_bias, w_out_e, w_in_o, sg_ln_g, sg_ln_b, sg_w, sg_b, w_out_o, final_g):
    b, s, d = x.shape
    assert d == D_MODEL and s % 512 == 0 and s >= KEY_WIN
    x2d = x.reshape(b * s, d)
    row = lambda a: a.reshape(1, -1).astype(F32)

    rw, ga, q, k, v, gb = _in_proj_even(x2d, row(norm_g[0]), w_in_e[0].astype(BF16), row(shift_mu[0]),
                                        seq_len=s, tm=512)
    r3 = lambda a: a.reshape(b, s, a.shape[-1])
    ya = _rwkv_pipelined(r3(rw), r3(ga), row(rw_w0[0]), rw_w2[0], row(rw_a0[0]), rw_a2[0], row(rw_kk[0]),
                         row(rw_ka[0]), row(rw_rk[0]), row(rw_lnx_g[0]), row(rw_lnx_b[0]), nb=2, rb=256)
    yb = _attn(r3(q), r3(k), r3(v), r3(gb), _bias_tables(att_bias[0]), cb=8)

    sgb_full = jnp.repeat(sg_b[0].T, SG_WIDTH // SG_GROUPS, axis=1).astype(F32)
    out = _layer1(x2d, ya.reshape(b * s, -1), yb.reshape(b * s, -1), w_out_e[0].astype(BF16),
                  row(norm_g[1]), w_in_o[0].astype(BF16), row(sg_ln_g[0]), row(sg_ln_b[0]), sg_w[0],
                  sgb_full, w_out_o[0].astype(BF16), row(final_g), tm=512)
    return out.reshape(b, s, d)
```

```python
import functools
import math

import jax
import jax.numpy as jnp
import numpy as np
from jax import lax
from jax.experimental import pallas as pl
from jax.experimental.pallas import tpu as pltpu

F32 = jnp.float32
BF16 = jnp.bfloat16

D_MODEL = 1024
CHUNK = 64
RW_HEAD_DIM = 64
RW_WIDTH = 512
DECAY_LORA = 64
AAA_LORA = 64
AT_HEAD_DIM = 64
AT_WIDTH = 512
LEFT_CHUNKS = 8
REL_CLIP = 2 * CHUNK
SG_CHUNK = 128
SG_WIDTH = 1024
SG_GROUPS = 8
SHIFT_WIDTH = 3 * RW_WIDTH + DECAY_LORA + AAA_LORA
EVEN_IN = SHIFT_WIDTH + RW_WIDTH + 4 * AT_WIDTH
RMS_EPS = 1e-6
LN_EPS = 1e-5
GN_EPS = 64e-5
NEG_INF = -1e30
LOG2E = math.log2(math.e)

VMEM_LIMIT = 56 * 1024 * 1024

PAIR = 2 * RW_HEAD_DIM


def _dot(a, b):
    return jnp.dot(a.astype(BF16), b.astype(BF16), preferred_element_type=F32)


def _dot_nt(a, b):
    return lax.dot_general(a.astype(BF16), b.astype(BF16), (((1,), (1,)), ((), ())),
                           preferred_element_type=F32)


def _split(x):
    hi = x.astype(BF16)
    lo = (x - hi.astype(F32)).astype(BF16)
    return hi, lo


def _sigmoid(x):
    return 1.0 / (1.0 + jnp.exp(-x))


def _silu(x):
    return x * _sigmoid(x)


def _gelu_tanh(x):
    c = math.sqrt(2.0 / math.pi)
    return 0.5 * x * (1.0 + jnp.tanh(c * (x + 0.044715 * (x * x * x))))


def _rmsnorm(x, g):
    ms = jnp.mean(x * x, axis=-1, keepdims=True)
    return x * lax.rsqrt(ms + RMS_EPS) * g


def _col_chunks(lo, hi, width=256):
    out = []
    c = lo
    while c < hi:
        w = min(width, hi - c)
        out.append((c, w))
        c += w
    return out


def _in_proj_even_kernel(x_ref, g_ref, w_ref, mu_ref, rw_ref, ga_ref, q_ref, k_ref, v_ref, gb_ref,
                         n_ref, carry_ref, *, tiles_per_seq):
    tm = x_ref.shape[0]
    i = pl.program_id(0)
    n_ref[...] = _rmsnorm(x_ref[...], g_ref[...]).astype(BF16)

    @pl.when(i % tiles_per_seq == 0)
    def _():
        carry_ref[...] = jnp.zeros_like(carry_ref)

    for c0, cw in _col_chunks(0, SHIFT_WIDTH):
        pc = jnp.dot(n_ref[...], w_ref[:, c0:c0 + cw], preferred_element_type=F32)
        row0 = lax.broadcasted_iota(jnp.int32, (tm, cw), 0) == 0
        prev = jnp.where(row0, carry_ref[0:1, c0:c0 + cw], pltpu.roll(pc, 1, 0))
        carry_ref[0:1, c0:c0 + cw] = pc[tm - 1:tm, :]
        rw_ref[:, c0:c0 + cw] = pc + (prev - pc) * mu_ref[:, c0:c0 + cw]

    base = SHIFT_WIDTH
    for dst, scale in ((ga_ref, None), (q_ref, LOG2E / math.sqrt(AT_HEAD_DIM)), (k_ref, None),
                       (v_ref, None), (gb_ref, None)):
        for c0, cw in _col_chunks(0, RW_WIDTH):
            pc = jnp.dot(n_ref[...], w_ref[:, base + c0:base + c0 + cw], preferred_element_type=F32)
            if scale is not None:
                pc = pc * scale
            dst[:, c0:c0 + cw] = pc.astype(dst.dtype)
        base += RW_WIDTH


def _in_proj_even(x2d, g, w_bf16, mu, seq_len, tm):
    t = x2d.shape[0]
    row = lambda i: (i, 0)
    const = lambda i: (0, 0)
    kern = functools.partial(_in_proj_even_kernel, tiles_per_seq=seq_len // tm)
    return pl.pallas_call(
        kern,
        grid=(t // tm,),
        in_specs=[pl.BlockSpec((tm, D_MODEL), row),
                  pl.BlockSpec((1, D_MODEL), const),
                  pl.BlockSpec((D_MODEL, EVEN_IN), const),
                  pl.BlockSpec((1, SHIFT_WIDTH), const)],
        out_specs=[pl.BlockSpec((tm, SHIFT_WIDTH), row),
                   pl.BlockSpec((tm, RW_WIDTH), row),
                   pl.BlockSpec((tm, AT_WIDTH), row),
                   pl.BlockSpec((tm, AT_WIDTH), row),
                   pl.BlockSpec((tm, AT_WIDTH), row),
                   pl.BlockSpec((tm, AT_WIDTH), row)],
        out_shape=[jax.ShapeDtypeStruct((t, SHIFT_WIDTH), F32),
                   jax.ShapeDtypeStruct((t, RW_WIDTH), F32),
                   jax.ShapeDtypeStruct((t, AT_WIDTH), BF16),
                   jax.ShapeDtypeStruct((t, AT_WIDTH), BF16),
                   jax.ShapeDtypeStruct((t, AT_WIDTH), BF16),
                   jax.ShapeDtypeStruct((t, AT_WIDTH), F32)],
        scratch_shapes=[pltpu.VMEM((tm, D_MODEL), BF16),
                        pltpu.VMEM((8, SHIFT_WIDTH), F32)],
        compiler_params=pltpu.CompilerParams(dimension_semantics=("arbitrary",),
                                             vmem_limit_bytes=VMEM_LIMIT),
        name="in_proj_even",
    )(x2d, g, w_bf16, mu)


EXP_NEG_HALF = math.exp(-0.5)
RWKV_SEQS_PER_STEP = 2


def _rwkv_pipelined_kernel(rw_ref, ga_ref, w0_ref, w2_ref, a0_ref, a2_ref, kk_ref, ka_ref, rk_ref,
                           lg_ref, lb_ref, o_ref, z_ref):
    nb, rb = rw_ref.shape[0], rw_ref.shape[1]
    n_chunks = rb // CHUNK
    n_pairs = RW_WIDTH // PAIR
    H = RW_HEAD_DIM

    @pl.when(pl.program_id(1) == 0)
    def _():
        z_ref[...] = jnp.zeros_like(z_ref)

    ri = lax.broadcasted_iota(jnp.int32, (PAIR, PAIR), 0)
    ci = lax.broadcasted_iota(jnp.int32, (PAIR, PAIR), 1)
    same_head = (ri // H) == (ci // H)
    tril_bd = jnp.logical_and(same_head, ci <= ri)
    stril_bd = jnp.logical_and(same_head, ci < ri)
    eye_mask = ri == ci
    eye = jnp.where(eye_mask, 1.0, 0.0).astype(F32)
    blk = {sz: (ri // sz) == (ci // sz) for sz in (8, 16, 32, 64)}
    merge = {sz: jnp.logical_and(blk[2 * sz], jnp.logical_not(blk[sz])) for sz in (8, 16, 32)}
    tr = lax.broadcasted_iota(jnp.int32, (rb, rb), 0)
    tc = lax.broadcasted_iota(jnp.int32, (rb, rb), 1)
    chunk_tril = jnp.where(jnp.logical_and((tr // CHUNK) == (tc // CHUNK), tc <= tr), 1.0, 0.0).astype(BF16)
    lane_head = lax.broadcasted_iota(jnp.int32, (CHUNK, PAIR), 1) // H
    head0 = lane_head == 0
    head1 = lane_head == 1
    first_head = lax.broadcasted_iota(jnp.int32, (rb, PAIR), 1) < H
    zeros = jnp.zeros((PAIR, PAIR), F32)

    def head_sum(x):
        out = []
        for p in range(n_pairs):
            xs = x[:, p * PAIR:(p + 1) * PAIR]
            s0 = jnp.sum(jnp.where(first_head, xs, 0.0), axis=1, keepdims=True)
            s1 = jnp.sum(jnp.where(first_head, 0.0, xs), axis=1, keepdims=True)
            out.append(jnp.where(first_head, s0, s1))
        return jnp.concatenate(out, axis=1)

    def bd(x, cc, p):
        xs = x[cc * CHUNK:(cc + 1) * CHUNK, p * PAIR:(p + 1) * PAIR]
        return jnp.concatenate([jnp.where(head0, xs, 0.0), jnp.where(head1, xs, 0.0)], axis=0)

    def low_rows(x, sz):
        return jnp.concatenate([x[s:s + sz] for s in range(sz, PAIR, 2 * sz)], axis=0)

    def spread_low_rows(y, sz):
        zero = jnp.zeros((sz, y.shape[1]), y.dtype)
        parts = []
        for i in range(PAIR // (2 * sz)):
            parts += [zero, y[i * sz:(i + 1) * sz]]
        return jnp.concatenate(parts, axis=0)

    def blocks(top, bottom, mask):
        return jnp.where(mask, jnp.concatenate([top, bottom], axis=0), 0.0)

    chains = [(cc, p) for cc in range(n_chunks) for p in range(n_pairs)]
    st = [dict() for _ in range(nb)]

    def prep_lora(bi):
        s = st[bi]
        wd = rw_ref[bi, :, 3 * RW_WIDTH:3 * RW_WIDTH + DECAY_LORA]
        ad = rw_ref[bi, :, 3 * RW_WIDTH + DECAY_LORA:SHIFT_WIDTH]
        s["zw"] = w0_ref[...] + _dot(jnp.tanh(wd), w2_ref[...])
        s["za"] = a0_ref[...] + _dot(ad, a2_ref[...])

    def prep_decay(bi):
        s = st[bi]
        lw = -EXP_NEG_HALF * _sigmoid(s["zw"])
        lw_hi, lw_lo = _split(lw)
        s["lw"] = lw
        s["c"] = _dot(chunk_tril, lw_hi) + _dot(chunk_tril, lw_lo)

    def prep_rows(bi):
        s = st[bi]
        r = rw_ref[bi, :, 0:RW_WIDTH]
        k = rw_ref[bi, :, RW_WIDTH:2 * RW_WIDTH]
        v = rw_ref[bi, :, 2 * RW_WIDTH:3 * RW_WIDTH]
        c, lw = s["c"], s["lw"]
        c_last = jnp.concatenate(
            [jnp.broadcast_to(c[(cc + 1) * CHUNK - 1:(cc + 1) * CHUNK, :], (CHUNK, RW_WIDTH))
             for cc in range(n_chunks)], axis=0)
        e_c = jnp.exp(c)
        e_nc = jnp.exp(-c)
        e_cm = jnp.exp(c - lw)
        e_lc = jnp.exp(c_last - c)
        g_l = jnp.exp(c_last)
        a_g = _sigmoid(s["za"])
        kk = k * kk_ref[...]
        ss = head_sum(kk * kk)
        kkn = kk * lax.rsqrt(jnp.maximum(ss, 1e-24))
        k_m = k * (1.0 + (a_g - 1.0) * ka_ref[...])
        b_v = kkn * a_g
        rh = r * e_c
        kh = k_m * e_nc
        bh = b_v * e_nc
        ah = -kkn * e_cm
        bt = b_v * e_lc
        kt = k_m * e_lc
        s["bonus"] = head_sum(r * k_m * rk_ref[...]) * v
        for key in ("rh_bd", "ah_bd", "v_bd", "bt_t", "kt_t", "g_last", "g"):
            s[key] = {}
        for ch in chains:
            cc, p = ch
            rows = slice(cc * CHUNK, (cc + 1) * CHUNK)
            sl = slice(p * PAIR, (p + 1) * PAIR)
            s["rh_bd"][ch] = bd(rh, cc, p)
            s["ah_bd"][ch] = bd(ah, cc, p)
            s["v_bd"][ch] = bd(v, cc, p)
            s["bt_t"][ch] = bd(bt, cc, p).T
            s["kt_t"][ch] = bd(kt, cc, p).T
            s["g_last"][ch] = g_l[cc * CHUNK:cc * CHUNK + 1, sl]
            lhs = jnp.concatenate([s["rh_bd"][ch], s["ah_bd"][ch]], axis=0)
            rhs = jnp.concatenate([kh[rows, sl], bh[rows, sl]], axis=0)
            s["g"][ch] = _dot_nt(lhs, rhs)

    def m_blocks(bi):
        s = st[bi]
        g = s.pop("g")
        gs = {ch: pltpu.roll(g[ch], H, 1) for ch in chains}
        s["a_rk"] = {ch: blocks(g[ch][0:H], gs[ch][H:2 * H], tril_bd) for ch in chains}
        s["a_rb"] = {ch: blocks(gs[ch][0:H], g[ch][H:2 * H], tril_bd) for ch in chains}
        s["a_ak"] = {ch: blocks(g[ch][2 * H:3 * H], gs[ch][3 * H:4 * H], stril_bd) for ch in chains}
        s["a_ab"] = {ch: blocks(gs[ch][2 * H:3 * H], g[ch][3 * H:4 * H], stril_bd) for ch in chains}
        s["a8"] = {ch: jnp.where(blk[8], s["a_ab"][ch], 0.0) for ch in chains}
        s["a2"] = {ch: _dot(s["a8"][ch], s["a8"][ch]) for ch in chains}

    def m_neumann2(bi):
        s = st[bi]
        b1 = {ch: eye + s["a8"][ch] for ch in chains}
        s["b2"] = {ch: b1[ch] + _dot(s["a2"][ch], b1[ch]) for ch in chains}
        s["a4"] = {ch: _dot(s["a2"][ch], s["a2"][ch]) for ch in chains}

    def m_neumann3(bi):
        s = st[bi]
        s["t"] = {ch: s["b2"][ch] + _dot(s["a4"][ch], s["b2"][ch]) for ch in chains}

    def m_merge_a(sz):
        def f(bi):
            s = st[bi]
            s["ed"] = {ch: _dot(low_rows(jnp.where(merge[sz], s["a_ab"][ch], 0.0), sz), s["t"][ch])
                       for ch in chains}
        return f

    def m_merge_b(sz):
        def f(bi):
            s = st[bi]
            upd = {ch: _dot(low_rows(s["t"][ch], sz), spread_low_rows(s["ed"][ch], sz)) for ch in chains}
            s["t"] = {ch: s["t"][ch] + spread_low_rows(upd[ch], sz) for ch in chains}
        return f

    def m_x1(bi):
        s = st[bi]
        s["x1"] = {ch: _dot(s["a_ak"][ch], s["v_bd"][ch]) for ch in chains}

    def m_wu(bi):
        s = st[bi]
        s["wu"] = {ch: _dot(s["t"][ch], jnp.concatenate([s["ah_bd"][ch], s["x1"][ch]], axis=1))
                   for ch in chains}

    def m_qymn(bi):
        s = st[bi]
        s["qymn"] = {}
        for ch in chains:
            lhs = jnp.concatenate([jnp.concatenate([s["a_rb"][ch], s["a_rk"][ch]], axis=1),
                                   jnp.concatenate([s["bt_t"][ch], s["kt_t"][ch]], axis=1)], axis=0)
            rhs = jnp.concatenate([s["wu"][ch], jnp.concatenate([zeros, s["v_bd"][ch]], axis=1)], axis=0)
            s["qymn"][ch] = _dot(lhs, rhs)
        s["z"] = [z_ref[bi, p] for p in range(n_pairs)]
        s["y_rows"] = []

    def tail(cc):
        def f(bi):
            s = st[bi]
            y_pairs = []
            for p in range(n_pairs):
                ch = (cc, p)
                qymn = s["qymn"][ch]
                q = s["rh_bd"][ch] + qymn[0:PAIR, 0:PAIR]
                m = qymn[PAIR:2 * PAIR, 0:PAIR] + jnp.where(eye_mask, s["g_last"][ch], 0.0)
                zy = _dot(jnp.concatenate([m, q], axis=0), s["z"][p])
                s["z"][p] = zy[0:PAIR] + qymn[PAIR:2 * PAIR, PAIR:2 * PAIR]
                y_bd = zy[PAIR:2 * PAIR] + qymn[0:PAIR, PAIR:2 * PAIR]
                y_pairs.append(y_bd[0:CHUNK] + y_bd[CHUNK:PAIR])
            s["y_rows"].append(jnp.concatenate(y_pairs, axis=1))
        return f

    def epilogue(bi):
        s = st[bi]
        for p in range(n_pairs):
            z_ref[bi, p] = s["z"][p]
        y = jnp.concatenate(s["y_rows"], axis=0)
        mu = head_sum(y) * (1.0 / H)
        d = y - mu
        var = head_sum(d * d) * (1.0 / H)
        yn = d * lax.rsqrt(var + GN_EPS) * lg_ref[...] + lb_ref[...] + s["bonus"]
        o_ref[bi] = (yn * _silu(ga_ref[bi])).astype(o_ref.dtype)
        s.clear()

    prep = [prep_lora, prep_decay, prep_rows]
    middle = [m_blocks, m_neumann2, m_neumann3]
    for sz in (8, 16, 32):
        middle += [m_merge_a(sz), m_merge_b(sz)]
    middle += [m_x1, m_wu, m_qymn]
    finish = [tail(cc) for cc in range(n_chunks)] + [epilogue]

    for f in prep:
        f(0)
    for bi in range(nb):
        side = []
        if bi > 0:
            side += [(g, bi - 1) for g in finish]
        if bi + 1 < nb:
            side += [(g, bi + 1) for g in prep]
        every = max(1, len(middle) // (len(side) + 1)) if side else 0
        for i, f in enumerate(middle):
            f(bi)
            if side and (i + 1) % every == 0:
                g, other = side.pop(0)
                g(other)
        for g, other in side:
            g(other)
    for g in finish:
        g(nb - 1)


def _rwkv_pipelined(rw, ga, w0, w2, a0, a2, k_k, k_a, r_k, lnx_g, lnx_b, nb, rb):
    b, s, _ = rw.shape
    blk3 = lambda i, j: (i, j, 0)
    const = lambda i, j: (0, 0)
    vec = pl.BlockSpec((1, RW_WIDTH), const)
    lora = pl.BlockSpec((DECAY_LORA, RW_WIDTH), const)
    return pl.pallas_call(
        _rwkv_pipelined_kernel,
        grid=(b // nb, s // rb),
        in_specs=[pl.BlockSpec((nb, rb, SHIFT_WIDTH), blk3),
                  pl.BlockSpec((nb, rb, RW_WIDTH), blk3),
                  vec, lora, vec, lora, vec, vec, vec, vec, vec],
        out_specs=pl.BlockSpec((nb, rb, RW_WIDTH), blk3),
        out_shape=jax.ShapeDtypeStruct((b, s, RW_WIDTH), BF16),
        scratch_shapes=[pltpu.VMEM((nb, RW_WIDTH // PAIR, PAIR, PAIR), F32)],
        compiler_params=pltpu.CompilerParams(dimension_semantics=("parallel", "arbitrary"),
                                             vmem_limit_bytes=VMEM_LIMIT),
        name="rwkv7_mix",
    )(rw, ga, w0, w2, a0, a2, k_k, k_a, r_k, lnx_g, lnx_b)


SLAB = 4 * AT_HEAD_DIM
KEY_WIN = (LEFT_CHUNKS + 2) * CHUNK
N_BIAS = LEFT_CHUNKS + 2


def _attn_kernel(q_ref, k_ref, v_ref, gb_ref, bias_ref, o_ref):
    cb = q_ref.shape[0] // CHUNK
    j = pl.program_id(1)
    lane_head = lax.broadcasted_iota(jnp.int32, (CHUNK, SLAB), 1) // AT_HEAD_DIM
    head_masks = [lane_head == h for h in range(SLAB // AT_HEAD_DIM)]
    for cc in range(cb):
        c = j * cb + cc
        start = pl.multiple_of(jnp.maximum(c - (LEFT_CHUNKS + 1), 0) * CHUNK, CHUNK)
        table = jnp.minimum(c, N_BIAS - 1)
        rows = slice(cc * CHUNK, (cc + 1) * CHUNK)
        for s0 in range(0, AT_WIDTH, SLAB):
            cols = slice(s0, s0 + SLAB)
            qs = q_ref[rows, cols]
            q_bd = jnp.concatenate([jnp.where(mh, qs, jnp.zeros_like(qs)) for mh in head_masks], axis=0)
            kb = k_ref[pl.ds(start, KEY_WIN), cols]
            s = lax.dot_general(q_bd, kb, (((1,), (1,)), ((), ())), preferred_element_type=F32)
            s = s + bias_ref[table, s0:s0 + SLAB, :]
            mx = jnp.max(s, axis=1, keepdims=True)
            e = jnp.exp2(s - mx)
            l = jnp.sum(e, axis=1, keepdims=True)
            vb = v_ref[pl.ds(start, KEY_WIN), cols]
            o_bd = jnp.dot(e.astype(BF16), vb, preferred_element_type=F32) * (1.0 / l)
            o = jnp.zeros((CHUNK, SLAB), F32)
            for h, mh in enumerate(head_masks):
                o = jnp.where(mh, o_bd[h * CHUNK:(h + 1) * CHUNK, :], o)
            o_ref[rows, cols] = (o * _silu(gb_ref[rows, cols])).astype(o_ref.dtype)


def _attn(q, k, v, gb, bias_tables, cb):
    b, s, _ = q.shape
    qblk = lambda i, j: (i, j, 0)
    full = lambda i, j: (i, 0, 0)
    return pl.pallas_call(
        _attn_kernel,
        grid=(b, s // (cb * CHUNK)),
        in_specs=[pl.BlockSpec((None, cb * CHUNK, AT_WIDTH), qblk),
                  pl.BlockSpec((None, s, AT_WIDTH), full),
                  pl.BlockSpec((None, s, AT_WIDTH), full),
                  pl.BlockSpec((None, cb * CHUNK, AT_WIDTH), qblk),
                  pl.BlockSpec(bias_tables.shape, lambda i, j: (0, 0, 0))],
        out_specs=pl.BlockSpec((None, cb * CHUNK, AT_WIDTH), qblk),
        out_shape=jax.ShapeDtypeStruct((b, s, AT_WIDTH), BF16),
        compiler_params=pltpu.CompilerParams(dimension_semantics=("parallel", "arbitrary"),
                                             vmem_limit_bytes=VMEM_LIMIT),
        name="chunk_attention",
    )(q, k, v, gb, bias_tables)


GMLP_SUB = 256


def _layer1_kernel(x_ref, ya_ref, yb_ref, woe_ref, g1_ref, wio_ref, lng_ref, lnb_ref, sgw_ref,
                   sgb_ref, woo_ref, fg_ref, o_ref):
    tm = x_ref.shape[0]
    gd = SG_WIDTH // SG_GROUPS
    pr = lax.broadcasted_iota(jnp.int32, (SG_CHUNK, SG_CHUNK), 0) // CHUNK
    pc = lax.broadcasted_iota(jnp.int32, (SG_CHUNK, SG_CHUNK), 1) // CHUNK
    causal = pc <= pr
    wg = [jnp.where(causal, sgw_ref[g], 0.0).astype(BF16) for g in range(SG_GROUPS)]

    subs = [slice(i * GMLP_SUB, (i + 1) * GMLP_SUB) for i in range(tm // GMLP_SUB)]
    h1 = [x_ref[rs, :] + jnp.dot(ya_ref[rs, :], woe_ref[0:RW_WIDTH, :], preferred_element_type=F32)
          + jnp.dot(yb_ref[rs, :], woe_ref[RW_WIDTH:, :], preferred_element_type=F32) for rs in subs]
    n1 = [_rmsnorm(h, g1_ref[...]).astype(BF16) for h in h1]
    vv = [_gelu_tanh(jnp.dot(n, wio_ref[:, SG_WIDTH:2 * SG_WIDTH], preferred_element_type=F32)) for n in n1]
    u = [_gelu_tanh(jnp.dot(n, wio_ref[:, 0:SG_WIDTH], preferred_element_type=F32)) for n in n1]
    gate = [jnp.dot(n, wio_ref[:, 2 * SG_WIDTH:], preferred_element_type=F32) for n in n1]
    vln = []
    for x in vv:
        mu = jnp.mean(x, axis=-1, keepdims=True)
        dv = x - mu
        var = jnp.mean(dv * dv, axis=-1, keepdims=True)
        vln.append((dv * lax.rsqrt(var + LN_EPS) * lng_ref[...] + lnb_ref[...]).astype(BF16))
    sv = []
    for xl in vln:
        cols = [jnp.concatenate([jnp.dot(wg[g], xl[nb * SG_CHUNK:(nb + 1) * SG_CHUNK, g * gd:(g + 1) * gd],
                                         preferred_element_type=F32) + sgb_ref[:, g * gd:(g + 1) * gd]
                                 for nb in range(GMLP_SUB // SG_CHUNK)], axis=0) for g in range(SG_GROUPS)]
        sv.append(jnp.concatenate(cols, axis=1))
    y = [((ui * svi) * _silu(gi)).astype(BF16) for ui, svi, gi in zip(u, sv, gate)]
    for rs, h, yi in zip(subs, h1, y):
        h2 = h + jnp.dot(yi, woo_ref[...], preferred_element_type=F32)
        o_ref[rs, :] = _rmsnorm(h2, fg_ref[...])


def _layer1(x2d, ya, yb, woe, g1, wio, lng, lnb, sgw, sgb_full, woo, fg, tm):
    t = x2d.shape[0]
    row = lambda i: (i, 0)
    const = lambda i: (0, 0)
    vec = pl.BlockSpec((1, D_MODEL), const)
    return pl.pallas_call(
        _layer1_kernel,
        grid=(t // tm,),
        in_specs=[pl.BlockSpec((tm, D_MODEL), row),
                  pl.BlockSpec((tm, RW_WIDTH), row),
                  pl.BlockSpec((tm, AT_WIDTH), row),
                  pl.BlockSpec((D_MODEL, D_MODEL), const),
                  vec,
                  pl.BlockSpec((D_MODEL, 3 * SG_WIDTH), const),
                  vec, vec,
                  pl.BlockSpec((SG_GROUPS, SG_CHUNK, SG_CHUNK), lambda i: (0, 0, 0)),
                  pl.BlockSpec((SG_CHUNK, SG_WIDTH), const),
                  pl.BlockSpec((SG_WIDTH, D_MODEL), const),
                  vec],
        out_specs=pl.BlockSpec((tm, D_MODEL), row),
        out_shape=jax.ShapeDtypeStruct((t, D_MODEL), F32),
        compiler_params=pltpu.CompilerParams(dimension_semantics=("parallel",),
                                             vmem_limit_bytes=VMEM_LIMIT),
        name="gmlp_layer",
    )(x2d, ya, yb, woe, g1, wio, lng, lnb, sgw, sgb_full, woo, fg)


def _bias_tables(att_bias):
    shift = np.array([c * CHUNK for c in range(N_BIAS - 1)] + [(LEFT_CHUNKS + 1) * CHUNK])
    n = np.arange(-(CHUNK - 1), KEY_WIN)
    idx = np.clip(shift[:, None] - n[None, :], -REL_CLIP, REL_CLIP) + REL_CLIP
    diag = jnp.transpose(att_bias[:, idx], (1, 0, 2)).astype(F32) * LOG2E
    w = diag.shape[-1]
    rep = jnp.tile(diag, (1, 1, CHUNK + 1))[:, :, :CHUNK * (w + 1)]
    bt = rep.reshape(N_BIAS, -1, CHUNK, w + 1)[:, :, ::-1, :KEY_WIN]
    kj = np.arange(KEY_WIN)
    lo = np.array([0] * (N_BIAS - 1) + [CHUNK])
    hi = np.array([(c + 1) * CHUNK for c in range(N_BIAS - 1)] + [KEY_WIN])
    valid = (kj[None, :] >= lo[:, None]) & (kj[None, :] < hi[:, None])
    bt = jnp.where(valid[:, None, None, :], bt, NEG_INF)
    return bt.reshape(N_BIAS, -1, KEY_WIN)


def kernel(x, norm_g, w_in_e, shift_mu, rw_w0, rw_w2, rw_a0, rw_a2, rw_kk, rw_ka, rw_rk, rw_lnx_g,
           rw_lnx_b, att_bias, w_out_e, w_in_o, sg_ln_g, sg_ln_b, sg_w, sg_b, w_out_o, final_g):
    b, s, d = x.shape
    assert d == D_MODEL and s % 512 == 0 and s >= KEY_WIN and b % RWKV_SEQS_PER_STEP == 0
    x2d = x.reshape(b * s, d)
    row = lambda a: a.reshape(1, -1).astype(F32)

    rw, ga, q, k, v, gb = _in_proj_even(x2d, row(norm_g[0]), w_in_e[0].astype(BF16), row(shift_mu[0]),
                                        seq_len=s, tm=512)
    r3 = lambda a: a.reshape(b, s, a.shape[-1])
    ya = _rwkv_pipelined(r3(rw), r3(ga), row(rw_w0[0]), rw_w2[0], row(rw_a0[0]), rw_a2[0], row(rw_kk[0]),
                         row(rw_ka[0]), row(rw_rk[0]), row(rw_lnx_g[0]), row(rw_lnx_b[0]),
                         nb=RWKV_SEQS_PER_STEP, rb=256)
    yb = _attn(r3(q), r3(k), r3(v), r3(gb), _bias_tables(att_bias[0]), cb=8)

    sgb_full = jnp.repeat(sg_b[0].T, SG_WIDTH // SG_GROUPS, axis=1).astype(F32)
    out = _layer1(x2d, ya.reshape(b * s, -1), yb.reshape(b * s, -1), w_out_e[0].astype(BF16),
                  row(norm_g[1]), w_in_o[0].astype(BF16), row(sg_ln_g[0]), row(sg_ln_b[0]), sg_w[0],
                  sgb_full, w_out_o[0].astype(BF16), row(final_g), tm=512)
    return out.reshape(b, s, d)
```

```python
import functools
import math

import jax
import jax.numpy as jnp
import numpy as np
from jax import lax
from jax.experimental import pallas as pl
from jax.experimental.pallas import tpu as pltpu

F32 = jnp.float32
BF16 = jnp.bfloat16

D_MODEL = 1024
CHUNK = 64
RW_HEAD_DIM = 64
RW_WIDTH = 512
DECAY_LORA = 64
AAA_LORA = 64
AT_HEAD_DIM = 64
AT_WIDTH = 512
LEFT_CHUNKS = 8
REL_CLIP = 2 * CHUNK
SG_CHUNK = 128
SG_WIDTH = 1024
SG_GROUPS = 8
SHIFT_WIDTH = 3 * RW_WIDTH + DECAY_LORA + AAA_LORA
EVEN_IN = SHIFT_WIDTH + RW_WIDTH + 4 * AT_WIDTH
RMS_EPS = 1e-6
LN_EPS = 1e-5
GN_EPS = 64e-5
NEG_INF = -1e30
LOG2E = math.log2(math.e)

VMEM_LIMIT = 56 * 1024 * 1024

PAIR = 2 * RW_HEAD_DIM


def _dot(a, b):
    return jnp.dot(a.astype(BF16), b.astype(BF16), preferred_element_type=F32)


def _dot_nt(a, b):
    return lax.dot_general(a.astype(BF16), b.astype(BF16), (((1,), (1,)), ((), ())),
                           preferred_element_type=F32)


def _split(x):
    hi = x.astype(BF16)
    lo = (x - hi.astype(F32)).astype(BF16)
    return hi, lo


def _sigmoid(x):
    return 1.0 / (1.0 + jnp.exp(-x))


def _silu(x):
    hx = 0.5 * x
    return hx + hx * jnp.tanh(hx)


def _gelu_tanh(x):
    c = math.sqrt(2.0 / math.pi)
    hx = 0.5 * x
    return hx + hx * jnp.tanh(x * (c + (c * 0.044715) * (x * x)))


def _rmsnorm(x, g):
    ms = jnp.mean(x * x, axis=-1, keepdims=True)
    return x * lax.rsqrt(ms + RMS_EPS) * g


def _col_chunks(lo, hi, width=256):
    out = []
    c = lo
    while c < hi:
        w = min(width, hi - c)
        out.append((c, w))
        c += w
    return out


def _in_proj_even_kernel(x_ref, g_ref, w_ref, mu_ref, rw_ref, ga_ref, q_ref, k_ref, v_ref, gb_ref,
                         n_ref, carry_ref, *, tiles_per_seq):
    tm = x_ref.shape[0]
    i = pl.program_id(0)
    n_ref[...] = _rmsnorm(x_ref[...], g_ref[...]).astype(BF16)

    @pl.when(i % tiles_per_seq == 0)
    def _():
        carry_ref[...] = jnp.zeros_like(carry_ref)

    for c0, cw in _col_chunks(0, SHIFT_WIDTH):
        pc = jnp.dot(n_ref[...], w_ref[:, c0:c0 + cw], preferred_element_type=F32)
        row0 = lax.broadcasted_iota(jnp.int32, (tm, cw), 0) == 0
        prev = jnp.where(row0, carry_ref[0:1, c0:c0 + cw], pltpu.roll(pc, 1, 0))
        carry_ref[0:1, c0:c0 + cw] = pc[tm - 1:tm, :]
        rw_ref[:, c0:c0 + cw] = pc + (prev - pc) * mu_ref[:, c0:c0 + cw]

    base = SHIFT_WIDTH
    for dst, scale in ((ga_ref, None), (q_ref, LOG2E / math.sqrt(AT_HEAD_DIM)), (k_ref, None),
                       (v_ref, None), (gb_ref, None)):
        for c0, cw in _col_chunks(0, RW_WIDTH):
            pc = jnp.dot(n_ref[...], w_ref[:, base + c0:base + c0 + cw], preferred_element_type=F32)
            if scale is not None:
                pc = pc * scale
            dst[:, c0:c0 + cw] = pc.astype(dst.dtype)
        base += RW_WIDTH


def _in_proj_even(x2d, g, w_bf16, mu, seq_len, tm):
    t = x2d.shape[0]
    row = lambda i: (i, 0)
    const = lambda i: (0, 0)
    kern = functools.partial(_in_proj_even_kernel, tiles_per_seq=seq_len // tm)
    return pl.pallas_call(
        kern,
        grid=(t // tm,),
        in_specs=[pl.BlockSpec((tm, D_MODEL), row),
                  pl.BlockSpec((1, D_MODEL), const),
                  pl.BlockSpec((D_MODEL, EVEN_IN), const),
                  pl.BlockSpec((1, SHIFT_WIDTH), const)],
        out_specs=[pl.BlockSpec((tm, SHIFT_WIDTH), row),
                   pl.BlockSpec((tm, RW_WIDTH), row),
                   pl.BlockSpec((tm, AT_WIDTH), row),
                   pl.BlockSpec((tm, AT_WIDTH), row),
                   pl.BlockSpec((tm, AT_WIDTH), row),
                   pl.BlockSpec((tm, AT_WIDTH), row)],
        out_shape=[jax.ShapeDtypeStruct((t, SHIFT_WIDTH), F32),
                   jax.ShapeDtypeStruct((t, RW_WIDTH), F32),
                   jax.ShapeDtypeStruct((t, AT_WIDTH), BF16),
                   jax.ShapeDtypeStruct((t, AT_WIDTH), BF16),
                   jax.ShapeDtypeStruct((t, AT_WIDTH), BF16),
                   jax.ShapeDtypeStruct((t, AT_WIDTH), F32)],
        scratch_shapes=[pltpu.VMEM((tm, D_MODEL), BF16),
                        pltpu.VMEM((8, SHIFT_WIDTH), F32)],
        compiler_params=pltpu.CompilerParams(dimension_semantics=("arbitrary",),
                                             vmem_limit_bytes=VMEM_LIMIT),
        name="in_proj_even",
    )(x2d, g, w_bf16, mu)


EXP_NEG_HALF = math.exp(-0.5)
RWKV_SEQS_PER_STEP = 2


def _rwkv_pipelined_kernel(rw_ref, ga_ref, w0_ref, w2_ref, a0_ref, a2_ref, kk_ref, ka_ref, rk_ref,
                           lg_ref, lb_ref, o_ref, z_ref):
    nb, rb = rw_ref.shape[0], rw_ref.shape[1]
    n_chunks = rb // CHUNK
    n_pairs = RW_WIDTH // PAIR
    H = RW_HEAD_DIM

    @pl.when(pl.program_id(1) == 0)
    def _():
        z_ref[...] = jnp.zeros_like(z_ref)

    ri = lax.broadcasted_iota(jnp.int32, (PAIR, PAIR), 0)
    ci = lax.broadcasted_iota(jnp.int32, (PAIR, PAIR), 1)
    same_head = (ri // H) == (ci // H)
    tril_bd = jnp.logical_and(same_head, ci <= ri)
    stril_bd = jnp.logical_and(same_head, ci < ri)
    eye_mask = ri == ci
    eye = jnp.where(eye_mask, 1.0, 0.0).astype(F32)
    blk = {sz: (ri // sz) == (ci // sz) for sz in (8, 16, 32, 64)}
    merge = {sz: jnp.logical_and(blk[2 * sz], jnp.logical_not(blk[sz])) for sz in (8, 16, 32)}
    tr = lax.broadcasted_iota(jnp.int32, (rb, rb), 0)
    tc = lax.broadcasted_iota(jnp.int32, (rb, rb), 1)
    chunk_tril = jnp.where(jnp.logical_and((tr // CHUNK) == (tc // CHUNK), tc <= tr), 1.0, 0.0).astype(BF16)
    lane_head = lax.broadcasted_iota(jnp.int32, (CHUNK, PAIR), 1) // H
    head0 = lane_head == 0
    head1 = lane_head == 1
    first_head = lax.broadcasted_iota(jnp.int32, (rb, PAIR), 1) < H
    zeros = jnp.zeros((PAIR, PAIR), F32)

    def head_sum(x):
        out = []
        for p in range(n_pairs):
            xs = x[:, p * PAIR:(p + 1) * PAIR]
            s0 = jnp.sum(jnp.where(first_head, xs, 0.0), axis=1, keepdims=True)
            s1 = jnp.sum(jnp.where(first_head, 0.0, xs), axis=1, keepdims=True)
            out.append(jnp.where(first_head, s0, s1))
        return jnp.concatenate(out, axis=1)

    def bd(x, cc, p):
        xs = x[cc * CHUNK:(cc + 1) * CHUNK, p * PAIR:(p + 1) * PAIR]
        return jnp.concatenate([jnp.where(head0, xs, 0.0), jnp.where(head1, xs, 0.0)], axis=0)

    def low_rows(x, sz):
        return jnp.concatenate([x[s:s + sz] for s in range(sz, PAIR, 2 * sz)], axis=0)

    def spread_low_rows(y, sz):
        zero = jnp.zeros((sz, y.shape[1]), y.dtype)
        parts = []
        for i in range(PAIR // (2 * sz)):
            parts += [zero, y[i * sz:(i + 1) * sz]]
        return jnp.concatenate(parts, axis=0)

    def blocks(top, bottom, mask):
        return jnp.where(mask, jnp.concatenate([top, bottom], axis=0), 0.0)

    chains = [(cc, p) for cc in range(n_chunks) for p in range(n_pairs)]
    st = [dict() for _ in range(nb)]

    def prep_lora(bi):
        s = st[bi]
        wd = rw_ref[bi, :, 3 * RW_WIDTH:3 * RW_WIDTH + DECAY_LORA]
        ad = rw_ref[bi, :, 3 * RW_WIDTH + DECAY_LORA:SHIFT_WIDTH]
        s["zw"] = w0_ref[...] + _dot(jnp.tanh(wd), w2_ref[...])
        s["za"] = a0_ref[...] + _dot(ad, a2_ref[...])

    def prep_decay(bi):
        s = st[bi]
        lw = -EXP_NEG_HALF * _sigmoid(s["zw"])
        lw_hi, lw_lo = _split(lw)
        s["lw"] = lw
        s["c"] = _dot(chunk_tril, lw_hi) + _dot(chunk_tril, lw_lo)

    def prep_rows(bi):
        s = st[bi]
        r = rw_ref[bi, :, 0:RW_WIDTH]
        k = rw_ref[bi, :, RW_WIDTH:2 * RW_WIDTH]
        v = rw_ref[bi, :, 2 * RW_WIDTH:3 * RW_WIDTH]
        c, lw = s["c"], s["lw"]
        c_last = jnp.concatenate(
            [jnp.broadcast_to(c[(cc + 1) * CHUNK - 1:(cc + 1) * CHUNK, :], (CHUNK, RW_WIDTH))
             for cc in range(n_chunks)], axis=0)
        e_c = jnp.exp(c)
        e_nc = jnp.exp(-c)
        e_cm = jnp.exp(c - lw)
        e_lc = jnp.exp(c_last - c)
        g_l = jnp.exp(c_last)
        a_g = _sigmoid(s["za"])
        kk = k * kk_ref[...]
        ss = head_sum(kk * kk)
        kkn = kk * lax.rsqrt(jnp.maximum(ss, 1e-24))
        k_m = k * (1.0 + (a_g - 1.0) * ka_ref[...])
        b_v = kkn * a_g
        rh = r * e_c
        kh = k_m * e_nc
        bh = b_v * e_nc
        ah = -kkn * e_cm
        bt = b_v * e_lc
        kt = k_m * e_lc
        s["bonus"] = head_sum(r * k_m * rk_ref[...]) * v
        for key in ("rh_bd", "ah_bd", "v_bd", "bt_t", "kt_t", "g_last", "g"):
            s[key] = {}
        for ch in chains:
            cc, p = ch
            rows = slice(cc * CHUNK, (cc + 1) * CHUNK)
            sl = slice(p * PAIR, (p + 1) * PAIR)
            s["rh_bd"][ch] = bd(rh, cc, p)
            s["ah_bd"][ch] = bd(ah, cc, p)
            s["v_bd"][ch] = bd(v, cc, p)
            s["bt_t"][ch] = bd(bt, cc, p).T
            s["kt_t"][ch] = bd(kt, cc, p).T
            s["g_last"][ch] = g_l[cc * CHUNK:cc * CHUNK + 1, sl]
            lhs = jnp.concatenate([s["rh_bd"][ch], s["ah_bd"][ch]], axis=0)
            rhs = jnp.concatenate([kh[rows, sl], bh[rows, sl]], axis=0)
            s["g"][ch] = _dot_nt(lhs, rhs)

    def m_blocks(bi):
        s = st[bi]
        g = s.pop("g")
        gs = {ch: pltpu.roll(g[ch], H, 1) for ch in chains}
        s["a_rk"] = {ch: blocks(g[ch][0:H], gs[ch][H:2 * H], tril_bd) for ch in chains}
        s["a_rb"] = {ch: blocks(gs[ch][0:H], g[ch][H:2 * H], tril_bd) for ch in chains}
        s["a_ak"] = {ch: blocks(g[ch][2 * H:3 * H], gs[ch][3 * H:4 * H], stril_bd) for ch in chains}
        s["a_ab"] = {ch: blocks(gs[ch][2 * H:3 * H], g[ch][3 * H:4 * H], stril_bd) for ch in chains}
        s["a8"] = {ch: jnp.where(blk[8], s["a_ab"][ch], 0.0) for ch in chains}
        s["a2"] = {ch: _dot(s["a8"][ch], s["a8"][ch]) for ch in chains}

    def m_neumann2(bi):
        s = st[bi]
        b1 = {ch: eye + s["a8"][ch] for ch in chains}
        s["b2"] = {ch: b1[ch] + _dot(s["a2"][ch], b1[ch]) for ch in chains}
        s["a4"] = {ch: _dot(s["a2"][ch], s["a2"][ch]) for ch in chains}

    def m_neumann3(bi):
        s = st[bi]
        s["t"] = {ch: s["b2"][ch] + _dot(s["a4"][ch], s["b2"][ch]) for ch in chains}

    def m_merge_a(sz):
        def f(bi):
            s = st[bi]
            s["ed"] = {ch: _dot(low_rows(jnp.where(merge[sz], s["a_ab"][ch], 0.0), sz), s["t"][ch])
                       for ch in chains}
        return f

    def m_merge_b(sz):
        def f(bi):
            s = st[bi]
            upd = {ch: _dot(low_rows(s["t"][ch], sz), spread_low_rows(s["ed"][ch], sz)) for ch in chains}
            s["t"] = {ch: s["t"][ch] + spread_low_rows(upd[ch], sz) for ch in chains}
        return f

    def m_x1(bi):
        s = st[bi]
        s["x1"] = {ch: _dot(s["a_ak"][ch], s["v_bd"][ch]) for ch in chains}

    def m_wu(bi):
        s = st[bi]
        s["wu"] = {ch: _dot(s["t"][ch], jnp.concatenate([s["ah_bd"][ch], s["x1"][ch]], axis=1))
                   for ch in chains}

    def m_qymn(bi):
        s = st[bi]
        s["qymn"] = {}
        for ch in chains:
            lhs = jnp.concatenate([jnp.concatenate([s["a_rb"][ch], s["a_rk"][ch]], axis=1),
                                   jnp.concatenate([s["bt_t"][ch], s["kt_t"][ch]], axis=1)], axis=0)
            rhs = jnp.concatenate([s["wu"][ch], jnp.concatenate([zeros, s["v_bd"][ch]], axis=1)], axis=0)
            s["qymn"][ch] = _dot(lhs, rhs)
        s["z"] = [z_ref[bi, p] for p in range(n_pairs)]
        s["y_rows"] = []

    def tail(cc):
        def f(bi):
            s = st[bi]
            y_pairs = []
            for p in range(n_pairs):
                ch = (cc, p)
                qymn = s["qymn"][ch]
                q = s["rh_bd"][ch] + qymn[0:PAIR, 0:PAIR]
                m = qymn[PAIR:2 * PAIR, 0:PAIR] + jnp.where(eye_mask, s["g_last"][ch], 0.0)
                zy = _dot(jnp.concatenate([m, q], axis=0), s["z"][p])
                s["z"][p] = zy[0:PAIR] + qymn[PAIR:2 * PAIR, PAIR:2 * PAIR]
                y_bd = zy[PAIR:2 * PAIR] + qymn[0:PAIR, PAIR:2 * PAIR]
                y_pairs.append(y_bd[0:CHUNK] + y_bd[CHUNK:PAIR])
            s["y_rows"].append(jnp.concatenate(y_pairs, axis=1))
        return f

    def epilogue(bi):
        s = st[bi]
        for p in range(n_pairs):
            z_ref[bi, p] = s["z"][p]
        y = jnp.concatenate(s["y_rows"], axis=0)
        mu = head_sum(y) * (1.0 / H)
        d = y - mu
        var = head_sum(d * d) * (1.0 / H)
        yn = d * lax.rsqrt(var + GN_EPS) * lg_ref[...] + lb_ref[...] + s["bonus"]
        o_ref[bi] = (yn * _silu(ga_ref[bi])).astype(o_ref.dtype)
        s.clear()

    prep = [prep_lora, prep_decay, prep_rows]
    middle = [m_blocks, m_neumann2, m_neumann3]
    for sz in (8, 16, 32):
        middle += [m_merge_a(sz), m_merge_b(sz)]
    middle += [m_x1, m_wu, m_qymn]
    finish = [tail(cc) for cc in range(n_chunks)] + [epilogue]

    for f in prep:
        f(0)
    for bi in range(nb):
        side = []
        if bi > 0:
            side += [(g, bi - 1) for g in finish]
        if bi + 1 < nb:
            side += [(g, bi + 1) for g in prep]
        every = max(1, len(middle) // (len(side) + 1)) if side else 0
        for i, f in enumerate(middle):
            f(bi)
            if side and (i + 1) % every == 0:
                g, other = side.pop(0)
                g(other)
        for g, other in side:
            g(other)
    for g in finish:
        g(nb - 1)


def _rwkv_pipelined(rw, ga, w0, w2, a0, a2, k_k, k_a, r_k, lnx_g, lnx_b, nb, rb):
    b, s, _ = rw.shape
    blk3 = lambda i, j: (i, j, 0)
    const = lambda i, j: (0, 0)
    vec = pl.BlockSpec((1, RW_WIDTH), const)
    lora = pl.BlockSpec((DECAY_LORA, RW_WIDTH), const)
    return pl.pallas_call(
        _rwkv_pipelined_kernel,
        grid=(b // nb, s // rb),
        in_specs=[pl.BlockSpec((nb, rb, SHIFT_WIDTH), blk3),
                  pl.BlockSpec((nb, rb, RW_WIDTH), blk3),
                  vec, lora, vec, lora, vec, vec, vec, vec, vec],
        out_specs=pl.BlockSpec((nb, rb, RW_WIDTH), blk3),
        out_shape=jax.ShapeDtypeStruct((b, s, RW_WIDTH), BF16),
        scratch_shapes=[pltpu.VMEM((nb, RW_WIDTH // PAIR, PAIR, PAIR), F32)],
        compiler_params=pltpu.CompilerParams(dimension_semantics=("parallel", "arbitrary"),
                                             vmem_limit_bytes=VMEM_LIMIT),
        name="rwkv7_mix",
    )(rw, ga, w0, w2, a0, a2, k_k, k_a, r_k, lnx_g, lnx_b)


SLAB = 4 * AT_HEAD_DIM
KEY_WIN = (LEFT_CHUNKS + 2) * CHUNK
N_BIAS = LEFT_CHUNKS + 2


def _attn_kernel(q_ref, k_ref, v_ref, gb_ref, bias_ref, o_ref):
    cb = q_ref.shape[0] // CHUNK
    j = pl.program_id(1)
    lane_head = lax.broadcasted_iota(jnp.int32, (CHUNK, SLAB), 1) // AT_HEAD_DIM
    head_masks = [lane_head == h for h in range(SLAB // AT_HEAD_DIM)]
    for cc in range(cb):
        c = j * cb + cc
        start = pl.multiple_of(jnp.maximum(c - (LEFT_CHUNKS + 1), 0) * CHUNK, CHUNK)
        table = jnp.minimum(c, N_BIAS - 1)
        rows = slice(cc * CHUNK, (cc + 1) * CHUNK)
        for s0 in range(0, AT_WIDTH, SLAB):
            cols = slice(s0, s0 + SLAB)
            qs = q_ref[rows, cols]
            q_bd = jnp.concatenate([jnp.where(mh, qs, jnp.zeros_like(qs)) for mh in head_masks], axis=0)
            kb = k_ref[pl.ds(start, KEY_WIN), cols]
            s = lax.dot_general(q_bd, kb, (((1,), (1,)), ((), ())), preferred_element_type=F32)
            s = s + bias_ref[table, s0:s0 + SLAB, :]
            mx = jnp.max(s, axis=1, keepdims=True)
            e = jnp.exp2(s - mx)
            l = jnp.sum(e, axis=1, keepdims=True)
            vb = v_ref[pl.ds(start, KEY_WIN), cols]
            o_bd = jnp.dot(e.astype(BF16), vb, preferred_element_type=F32) * (1.0 / l)
            o = jnp.zeros((CHUNK, SLAB), F32)
            for h, mh in enumerate(head_masks):
                o = jnp.where(mh, o_bd[h * CHUNK:(h + 1) * CHUNK, :], o)
            o_ref[rows, cols] = (o * _silu(gb_ref[rows, cols])).astype(o_ref.dtype)


def _attn(q, k, v, gb, bias_tables, cb):
    b, s, _ = q.shape
    qblk = lambda i, j: (i, j, 0)
    full = lambda i, j: (i, 0, 0)
    return pl.pallas_call(
        _attn_kernel,
        grid=(b, s // (cb * CHUNK)),
        in_specs=[pl.BlockSpec((None, cb * CHUNK, AT_WIDTH), qblk),
                  pl.BlockSpec((None, s, AT_WIDTH), full),
                  pl.BlockSpec((None, s, AT_WIDTH), full),
                  pl.BlockSpec((None, cb * CHUNK, AT_WIDTH), qblk),
                  pl.BlockSpec(bias_tables.shape, lambda i, j: (0, 0, 0))],
        out_specs=pl.BlockSpec((None, cb * CHUNK, AT_WIDTH), qblk),
        out_shape=jax.ShapeDtypeStruct((b, s, AT_WIDTH), BF16),
        compiler_params=pltpu.CompilerParams(dimension_semantics=("parallel", "arbitrary"),
                                             vmem_limit_bytes=VMEM_LIMIT),
        name="chunk_attention",
    )(q, k, v, gb, bias_tables)


GMLP_SUB = 256


def _layer1_kernel(x_ref, ya_ref, yb_ref, woe_ref, g1_ref, wio_ref, lng_ref, lnb_ref, sgw_ref,
                   sgb_ref, woo_ref, fg_ref, o_ref):
    tm = x_ref.shape[0]
    gd = SG_WIDTH // SG_GROUPS
    pr = lax.broadcasted_iota(jnp.int32, (SG_CHUNK, SG_CHUNK), 0) // CHUNK
    pc = lax.broadcasted_iota(jnp.int32, (SG_CHUNK, SG_CHUNK), 1) // CHUNK
    causal = pc <= pr
    wg = [jnp.where(causal, sgw_ref[g], 0.0).astype(BF16) for g in range(SG_GROUPS)]

    subs = [slice(i * GMLP_SUB, (i + 1) * GMLP_SUB) for i in range(tm // GMLP_SUB)]
    h1 = [x_ref[rs, :] + jnp.dot(ya_ref[rs, :], woe_ref[0:RW_WIDTH, :], preferred_element_type=F32)
          + jnp.dot(yb_ref[rs, :], woe_ref[RW_WIDTH:, :], preferred_element_type=F32) for rs in subs]
    n1 = [_rmsnorm(h, g1_ref[...]).astype(BF16) for h in h1]
    vv = [_gelu_tanh(jnp.dot(n, wio_ref[:, SG_WIDTH:2 * SG_WIDTH], preferred_element_type=F32)) for n in n1]
    u = [_gelu_tanh(jnp.dot(n, wio_ref[:, 0:SG_WIDTH], preferred_element_type=F32)) for n in n1]
    gate = [jnp.dot(n, wio_ref[:, 2 * SG_WIDTH:], preferred_element_type=F32) for n in n1]
    vln = []
    for x in vv:
        mu = jnp.mean(x, axis=-1, keepdims=True)
        dv = x - mu
        var = jnp.mean(dv * dv, axis=-1, keepdims=True)
        vln.append((dv * lax.rsqrt(var + LN_EPS) * lng_ref[...] + lnb_ref[...]).astype(BF16))
    sv = []
    for xl in vln:
        cols = [jnp.concatenate([jnp.dot(wg[g], xl[nb * SG_CHUNK:(nb + 1) * SG_CHUNK, g * gd:(g + 1) * gd],
                                         preferred_element_type=F32) + sgb_ref[:, g * gd:(g + 1) * gd]
                                 for nb in range(GMLP_SUB // SG_CHUNK)], axis=0) for g in range(SG_GROUPS)]
        sv.append(jnp.concatenate(cols, axis=1))
    y = [((ui * svi) * _silu(gi)).astype(BF16) for ui, svi, gi in zip(u, sv, gate)]
    for rs, h, yi in zip(subs, h1, y):
        h2 = h + jnp.dot(yi, woo_ref[...], preferred_element_type=F32)
        o_ref[rs, :] = _rmsnorm(h2, fg_ref[...])


def _layer1(x2d, ya, yb, woe, g1, wio, lng, lnb, sgw, sgb_full, woo, fg, tm):
    t = x2d.shape[0]
    row = lambda i: (i, 0)
    const = lambda i: (0, 0)
    vec = pl.BlockSpec((1, D_MODEL), const)
    return pl.pallas_call(
        _layer1_kernel,
        grid=(t // tm,),
        in_specs=[pl.BlockSpec((tm, D_MODEL), row),
                  pl.BlockSpec((tm, RW_WIDTH), row),
                  pl.BlockSpec((tm, AT_WIDTH), row),
                  pl.BlockSpec((D_MODEL, D_MODEL), const),
                  vec,
                  pl.BlockSpec((D_MODEL, 3 * SG_WIDTH), const),
                  vec, vec,
                  pl.BlockSpec((SG_GROUPS, SG_CHUNK, SG_CHUNK), lambda i: (0, 0, 0)),
                  pl.BlockSpec((SG_CHUNK, SG_WIDTH), const),
                  pl.BlockSpec((SG_WIDTH, D_MODEL), const),
                  vec],
        out_specs=pl.BlockSpec((tm, D_MODEL), row),
        out_shape=jax.ShapeDtypeStruct((t, D_MODEL), F32),
        compiler_params=pltpu.CompilerParams(dimension_semantics=("parallel",),
                                             vmem_limit_bytes=VMEM_LIMIT),
        name="gmlp_layer",
    )(x2d, ya, yb, woe, g1, wio, lng, lnb, sgw, sgb_full, woo, fg)


def _bias_tables(att_bias):
    shift = np.array([c * CHUNK for c in range(N_BIAS - 1)] + [(LEFT_CHUNKS + 1) * CHUNK])
    n = np.arange(-(CHUNK - 1), KEY_WIN)
    idx = np.clip(shift[:, None] - n[None, :], -REL_CLIP, REL_CLIP) + REL_CLIP
    diag = jnp.transpose(att_bias[:, idx], (1, 0, 2)).astype(F32) * LOG2E
    diag = jnp.pad(diag, ((0, 0), (0, 0), (0, DIAG_PAD - diag.shape[-1])))
    n_heads = att_bias.shape[0]
    return pl.pallas_call(
        _bias_unfold_kernel,
        grid=(N_BIAS,),
        in_specs=[pl.BlockSpec((None, n_heads, DIAG_PAD), lambda c: (c, 0, 0))],
        out_specs=pl.BlockSpec((None, n_heads * CHUNK, KEY_WIN), lambda c: (c, 0, 0)),
        out_shape=jax.ShapeDtypeStruct((N_BIAS, n_heads * CHUNK, KEY_WIN), F32),
        compiler_params=pltpu.CompilerParams(dimension_semantics=("parallel",)),
        name="bias_unfold",
    )(diag)


DIAG_PAD = 768


def _bias_unfold_kernel(diag_ref, o_ref):
    c = pl.program_id(0)
    kj = lax.broadcasted_iota(jnp.int32, (CHUNK, KEY_WIN), 1)
    late = c == N_BIAS - 1
    lo = jnp.where(late, CHUNK, 0)
    hi = jnp.where(late, KEY_WIN, (c + 1) * CHUNK)
    valid = jnp.logical_and(kj >= lo, kj < hi)
    for h in range(diag_ref.shape[0]):
        rows = jnp.broadcast_to(diag_ref[h:h + 1, :], (CHUNK, DIAG_PAD))
        unfolded = pltpu.roll(rows, DIAG_PAD - (CHUNK - 1), 1, stride=1, stride_axis=0)
        o_ref[h * CHUNK:(h + 1) * CHUNK, :] = jnp.where(valid, unfolded[:, :KEY_WIN], NEG_INF)


def kernel(x, norm_g, w_in_e, shift_mu, rw_w0, rw_w2, rw_a0, rw_a2, rw_kk, rw_ka, rw_rk, rw_lnx_g,
           rw_lnx_b, att_bias, w_out_e, w_in_o, sg_ln_g, sg_ln_b, sg_w, sg_b, w_out_o, final_g):
    b, s, d = x.shape
    assert d == D_MODEL and s % 512 == 0 and s >= KEY_WIN and b % RWKV_SEQS_PER_STEP == 0
    x2d = x.reshape(b * s, d)
    row = lambda a: a.reshape(1, -1).astype(F32)

    rw, ga, q, k, v, gb = _in_proj_even(x2d, row(norm_g[0]), w_in_e[0].astype(BF16), row(shift_mu[0]),
                                        seq_len=s, tm=512)
    r3 = lambda a: a.reshape(b, s, a.shape[-1])
    ya = _rwkv_pipelined(r3(rw), r3(ga), row(rw_w0[0]), rw_w2[0], row(rw_a0[0]), rw_a2[0], row(rw_kk[0]),
                         row(rw_ka[0]), row(rw_rk[0]), row(rw_lnx_g[0]), row(rw_lnx_b[0]),
                         nb=RWKV_SEQS_PER_STEP, rb=256)
    yb = _attn(r3(q), r3(k), r3(v), r3(gb), _bias_tables(att_bias[0]), cb=8)

    sgb_full = jnp.repeat(sg_b[0].T, SG_WIDTH // SG_GROUPS, axis=1).astype(F32)
    out = _layer1(x2d, ya.reshape(b * s, -1), yb.reshape(b * s, -1), w_out_e[0].astype(BF16),
                  row(norm_g[1]), w_in_o[0].astype(BF16), row(sg_ln_g[0]), row(sg_ln_b[0]), sg_w[0],
                  sgb_full, w_out_o[0].astype(BF16), row(final_g), tm=512)
    return out.reshape(b, s, d)
```

```python
import functools
import math

import jax
import jax.numpy as jnp
import numpy as np
from jax import lax
from jax.experimental import pallas as pl
from jax.experimental.pallas import tpu as pltpu

F32 = jnp.float32
BF16 = jnp.bfloat16

D_MODEL = 1024
CHUNK = 64
RW_HEAD_DIM = 64
RW_WIDTH = 512
DECAY_LORA = 64
AAA_LORA = 64
AT_HEAD_DIM = 64
AT_WIDTH = 512
LEFT_CHUNKS = 8
REL_CLIP = 2 * CHUNK
SG_CHUNK = 128
SG_WIDTH = 1024
SG_GROUPS = 8
SHIFT_WIDTH = 3 * RW_WIDTH + DECAY_LORA + AAA_LORA
EVEN_IN = SHIFT_WIDTH + RW_WIDTH + 4 * AT_WIDTH
RMS_EPS = 1e-6
LN_EPS = 1e-5
GN_EPS = 64e-5
NEG_INF = -1e30
LOG2E = math.log2(math.e)

VMEM_LIMIT = 56 * 1024 * 1024

PAIR = 2 * RW_HEAD_DIM


def _dot(a, b):
    return jnp.dot(a.astype(BF16), b.astype(BF16), preferred_element_type=F32)


def _dot_nt(a, b):
    return lax.dot_general(a.astype(BF16), b.astype(BF16), (((1,), (1,)), ((), ())),
                           preferred_element_type=F32)


def _split(x):
    hi = x.astype(BF16)
    lo = (x - hi.astype(F32)).astype(BF16)
    return hi, lo


def _sigmoid(x):
    return 1.0 / (1.0 + jnp.exp(-x))


def _silu(x):
    hx = 0.5 * x
    return hx + hx * jnp.tanh(hx)


def _gelu_tanh(x):
    c = math.sqrt(2.0 / math.pi)
    hx = 0.5 * x
    return hx + hx * jnp.tanh(x * (c + (c * 0.044715) * (x * x)))


def _rmsnorm(x, g):
    ms = jnp.mean(x * x, axis=-1, keepdims=True)
    return x * lax.rsqrt(ms + RMS_EPS) * g


def _col_chunks(lo, hi, width=256):
    out = []
    c = lo
    while c < hi:
        w = min(width, hi - c)
        out.append((c, w))
        c += w
    return out


def _in_proj_even_kernel(x_ref, g_ref, w_ref, mu_ref, rw_ref, ga_ref, q_ref, k_ref, v_ref, gb_ref,
                         n_ref, carry_ref, *, tiles_per_seq):
    tm = x_ref.shape[0]
    i = pl.program_id(0)
    n_ref[...] = _rmsnorm(x_ref[...], g_ref[...]).astype(BF16)

    @pl.when(i % tiles_per_seq == 0)
    def _():
        carry_ref[...] = jnp.zeros_like(carry_ref)

    for c0, cw in _col_chunks(0, SHIFT_WIDTH):
        pc = jnp.dot(n_ref[...], w_ref[:, c0:c0 + cw], preferred_element_type=F32)
        row0 = lax.broadcasted_iota(jnp.int32, (tm, cw), 0) == 0
        prev = jnp.where(row0, carry_ref[0:1, c0:c0 + cw], pltpu.roll(pc, 1, 0))
        carry_ref[0:1, c0:c0 + cw] = pc[tm - 1:tm, :]
        rw_ref[:, c0:c0 + cw] = pc + (prev - pc) * mu_ref[:, c0:c0 + cw]

    base = SHIFT_WIDTH
    for dst, scale in ((ga_ref, None), (q_ref, LOG2E / math.sqrt(AT_HEAD_DIM)), (k_ref, None),
                       (v_ref, None), (gb_ref, None)):
        for c0, cw in _col_chunks(0, RW_WIDTH):
            pc = jnp.dot(n_ref[...], w_ref[:, base + c0:base + c0 + cw], preferred_element_type=F32)
            if scale is not None:
                pc = pc * scale
            dst[:, c0:c0 + cw] = pc.astype(dst.dtype)
        base += RW_WIDTH


def _in_proj_even(x2d, g, w_bf16, mu, seq_len, tm):
    t = x2d.shape[0]
    row = lambda i: (i, 0)
    const = lambda i: (0, 0)
    kern = functools.partial(_in_proj_even_kernel, tiles_per_seq=seq_len // tm)
    return pl.pallas_call(
        kern,
        grid=(t // tm,),
        in_specs=[pl.BlockSpec((tm, D_MODEL), row),
                  pl.BlockSpec((1, D_MODEL), const),
                  pl.BlockSpec((D_MODEL, EVEN_IN), const),
                  pl.BlockSpec((1, SHIFT_WIDTH), const)],
        out_specs=[pl.BlockSpec((tm, SHIFT_WIDTH), row),
                   pl.BlockSpec((tm, RW_WIDTH), row),
                   pl.BlockSpec((tm, AT_WIDTH), row),
                   pl.BlockSpec((tm, AT_WIDTH), row),
                   pl.BlockSpec((tm, AT_WIDTH), row),
                   pl.BlockSpec((tm, AT_WIDTH), row)],
        out_shape=[jax.ShapeDtypeStruct((t, SHIFT_WIDTH), F32),
                   jax.ShapeDtypeStruct((t, RW_WIDTH), F32),
                   jax.ShapeDtypeStruct((t, AT_WIDTH), BF16),
                   jax.ShapeDtypeStruct((t, AT_WIDTH), BF16),
                   jax.ShapeDtypeStruct((t, AT_WIDTH), BF16),
                   jax.ShapeDtypeStruct((t, AT_WIDTH), F32)],
        scratch_shapes=[pltpu.VMEM((tm, D_MODEL), BF16),
                        pltpu.VMEM((8, SHIFT_WIDTH), F32)],
        compiler_params=pltpu.CompilerParams(dimension_semantics=("arbitrary",),
                                             vmem_limit_bytes=VMEM_LIMIT),
        name="in_proj_even",
    )(x2d, g, w_bf16, mu)


EXP_NEG_HALF = math.exp(-0.5)
RWKV_SEQS_PER_STEP = 2


def _rwkv_pipelined_kernel(rw_ref, ga_ref, w0_ref, w2_ref, a0_ref, a2_ref, kk_ref, ka_ref, rk_ref,
                           lg_ref, lb_ref, o_ref, z_ref):
    nb, rb = rw_ref.shape[0], rw_ref.shape[1]
    n_chunks = rb // CHUNK
    n_pairs = RW_WIDTH // PAIR
    H = RW_HEAD_DIM

    @pl.when(pl.program_id(1) == 0)
    def _():
        z_ref[...] = jnp.zeros_like(z_ref)

    ri = lax.broadcasted_iota(jnp.int32, (PAIR, PAIR), 0)
    ci = lax.broadcasted_iota(jnp.int32, (PAIR, PAIR), 1)
    same_head = (ri // H) == (ci // H)
    tril_bd = jnp.logical_and(same_head, ci <= ri)
    stril_bd = jnp.logical_and(same_head, ci < ri)
    eye_mask = ri == ci
    eye = jnp.where(eye_mask, 1.0, 0.0).astype(F32)
    blk = {sz: (ri // sz) == (ci // sz) for sz in (8, 16, 32, 64)}
    merge = {sz: jnp.logical_and(blk[2 * sz], jnp.logical_not(blk[sz])) for sz in (8, 16, 32)}
    tr = lax.broadcasted_iota(jnp.int32, (rb, rb), 0)
    tc = lax.broadcasted_iota(jnp.int32, (rb, rb), 1)
    chunk_tril = jnp.where(jnp.logical_and((tr // CHUNK) == (tc // CHUNK), tc <= tr), 1.0, 0.0).astype(BF16)
    lane_head = lax.broadcasted_iota(jnp.int32, (CHUNK, PAIR), 1) // H
    head0 = lane_head == 0
    head1 = lane_head == 1
    first_head = lax.broadcasted_iota(jnp.int32, (rb, PAIR), 1) < H
    zeros = jnp.zeros((PAIR, PAIR), F32)

    def head_sum(x):
        out = []
        for p in range(n_pairs):
            xs = x[:, p * PAIR:(p + 1) * PAIR]
            s0 = jnp.sum(jnp.where(first_head, xs, 0.0), axis=1, keepdims=True)
            s1 = jnp.sum(jnp.where(first_head, 0.0, xs), axis=1, keepdims=True)
            out.append(jnp.where(first_head, s0, s1))
        return jnp.concatenate(out, axis=1)

    def bd(x, cc, p):
        xs = x[cc * CHUNK:(cc + 1) * CHUNK, p * PAIR:(p + 1) * PAIR]
        return jnp.concatenate([jnp.where(head0, xs, 0.0), jnp.where(head1, xs, 0.0)], axis=0)

    def low_rows(x, sz):
        return jnp.concatenate([x[s:s + sz] for s in range(sz, PAIR, 2 * sz)], axis=0)

    def spread_low_rows(y, sz):
        zero = jnp.zeros((sz, y.shape[1]), y.dtype)
        parts = []
        for i in range(PAIR // (2 * sz)):
            parts += [zero, y[i * sz:(i + 1) * sz]]
        return jnp.concatenate(parts, axis=0)

    def blocks(top, bottom, mask):
        return jnp.where(mask, jnp.concatenate([top, bottom], axis=0), 0.0)

    chains = [(cc, p) for cc in range(n_chunks) for p in range(n_pairs)]
    st = [dict() for _ in range(nb)]

    def prep_lora(bi):
        s = st[bi]
        wd = rw_ref[bi, :, 3 * RW_WIDTH:3 * RW_WIDTH + DECAY_LORA]
        ad = rw_ref[bi, :, 3 * RW_WIDTH + DECAY_LORA:SHIFT_WIDTH]
        s["zw"] = w0_ref[...] + _dot(jnp.tanh(wd), w2_ref[...])
        s["za"] = a0_ref[...] + _dot(ad, a2_ref[...])

    def prep_decay(bi):
        s = st[bi]
        lw = -EXP_NEG_HALF * _sigmoid(s["zw"])
        lw_hi, lw_lo = _split(lw)
        s["lw"] = lw
        s["c"] = _dot(chunk_tril, lw_hi) + _dot(chunk_tril, lw_lo)

    def prep_rows(bi):
        s = st[bi]
        r = rw_ref[bi, :, 0:RW_WIDTH]
        k = rw_ref[bi, :, RW_WIDTH:2 * RW_WIDTH]
        v = rw_ref[bi, :, 2 * RW_WIDTH:3 * RW_WIDTH]
        c, lw = s["c"], s["lw"]
        c_last = jnp.concatenate(
            [jnp.broadcast_to(c[(cc + 1) * CHUNK - 1:(cc + 1) * CHUNK, :], (CHUNK, RW_WIDTH))
             for cc in range(n_chunks)], axis=0)
        e_c = jnp.exp(c)
        e_nc = jnp.exp(-c)
        e_cm = jnp.exp(c - lw)
        e_lc = jnp.exp(c_last - c)
        g_l = jnp.exp(c_last)
        a_g = _sigmoid(s["za"])
        kk = k * kk_ref[...]
        ss = head_sum(kk * kk)
        kkn = kk * lax.rsqrt(jnp.maximum(ss, 1e-24))
        k_m = k * (1.0 + (a_g - 1.0) * ka_ref[...])
        b_v = kkn * a_g
        rh = r * e_c
        kh = k_m * e_nc
        bh = b_v * e_nc
        ah = -kkn * e_cm
        bt = b_v * e_lc
        kt = k_m * e_lc
        s["bonus"] = head_sum(r * k_m * rk_ref[...]) * v
        for key in ("rh_bd", "ah_bd", "v_bd", "bt_t", "kt_t", "g_last", "g"):
            s[key] = {}
        for ch in chains:
            cc, p = ch
            rows = slice(cc * CHUNK, (cc + 1) * CHUNK)
            sl = slice(p * PAIR, (p + 1) * PAIR)
            s["rh_bd"][ch] = bd(rh, cc, p)
            s["ah_bd"][ch] = bd(ah, cc, p)
            s["v_bd"][ch] = bd(v, cc, p)
            s["bt_t"][ch] = bd(bt, cc, p).T
            s["kt_t"][ch] = bd(kt, cc, p).T
            s["g_last"][ch] = g_l[cc * CHUNK:cc * CHUNK + 1, sl]
            lhs = jnp.concatenate([s["rh_bd"][ch], s["ah_bd"][ch]], axis=0)
            rhs = jnp.concatenate([kh[rows, sl], bh[rows, sl]], axis=0)
            s["g"][ch] = _dot_nt(lhs, rhs)

    def m_blocks(bi):
        s = st[bi]
        g = s.pop("g")
        gs = {ch: pltpu.roll(g[ch], H, 1) for ch in chains}
        s["a_rk"] = {ch: blocks(g[ch][0:H], gs[ch][H:2 * H], tril_bd) for ch in chains}
        s["a_rb"] = {ch: blocks(gs[ch][0:H], g[ch][H:2 * H], tril_bd) for ch in chains}
        s["a_ak"] = {ch: blocks(g[ch][2 * H:3 * H], gs[ch][3 * H:4 * H], stril_bd) for ch in chains}
        s["a_ab"] = {ch: blocks(gs[ch][2 * H:3 * H], g[ch][3 * H:4 * H], stril_bd) for ch in chains}
        s["a8"] = {ch: jnp.where(blk[8], s["a_ab"][ch], 0.0) for ch in chains}
        s["a2"] = {ch: _dot(s["a8"][ch], s["a8"][ch]) for ch in chains}

    def m_neumann2(bi):
        s = st[bi]
        b1 = {ch: eye + s["a8"][ch] for ch in chains}
        s["b2"] = {ch: b1[ch] + _dot(s["a2"][ch], b1[ch]) for ch in chains}
        s["a4"] = {ch: _dot(s["a2"][ch], s["a2"][ch]) for ch in chains}

    def m_neumann3(bi):
        s = st[bi]
        s["t"] = {ch: s["b2"][ch] + _dot(s["a4"][ch], s["b2"][ch]) for ch in chains}

    def m_merge_a(sz):
        def f(bi):
            s = st[bi]
            s["ed"] = {ch: _dot(low_rows(jnp.where(merge[sz], s["a_ab"][ch], 0.0), sz), s["t"][ch])
                       for ch in chains}
        return f

    def m_merge_b(sz):
        def f(bi):
            s = st[bi]
            upd = {ch: _dot(low_rows(s["t"][ch], sz), spread_low_rows(s["ed"][ch], sz)) for ch in chains}
            s["t"] = {ch: s["t"][ch] + spread_low_rows(upd[ch], sz) for ch in chains}
        return f

    def m_x1(bi):
        s = st[bi]
        s["x1"] = {ch: _dot(s["a_ak"][ch], s["v_bd"][ch]) for ch in chains}

    def m_wu(bi):
        s = st[bi]
        s["wu"] = {ch: _dot(s["t"][ch], jnp.concatenate([s["ah_bd"][ch], s["x1"][ch]], axis=1))
                   for ch in chains}

    def m_qymn(bi):
        s = st[bi]
        s["qymn"] = {}
        for ch in chains:
            lhs = jnp.concatenate([jnp.concatenate([s["a_rb"][ch], s["a_rk"][ch]], axis=1),
                                   jnp.concatenate([s["bt_t"][ch], s["kt_t"][ch]], axis=1)], axis=0)
            rhs = jnp.concatenate([s["wu"][ch], jnp.concatenate([zeros, s["v_bd"][ch]], axis=1)], axis=0)
            s["qymn"][ch] = _dot(lhs, rhs)
        s["z"] = [z_ref[bi, p] for p in range(n_pairs)]
        s["y_rows"] = []

    def tail(cc):
        def f(bi):
            s = st[bi]
            y_pairs = []
            for p in range(n_pairs):
                ch = (cc, p)
                qymn = s["qymn"][ch]
                q = s["rh_bd"][ch] + qymn[0:PAIR, 0:PAIR]
                m = qymn[PAIR:2 * PAIR, 0:PAIR] + jnp.where(eye_mask, s["g_last"][ch], 0.0)
                zy = _dot(jnp.concatenate([m, q], axis=0), s["z"][p])
                s["z"][p] = zy[0:PAIR] + qymn[PAIR:2 * PAIR, PAIR:2 * PAIR]
                y_bd = zy[PAIR:2 * PAIR] + qymn[0:PAIR, PAIR:2 * PAIR]
                y_pairs.append(y_bd[0:CHUNK] + y_bd[CHUNK:PAIR])
            s["y_rows"].append(jnp.concatenate(y_pairs, axis=1))
        return f

    def epilogue(bi):
        s = st[bi]
        for p in range(n_pairs):
            z_ref[bi, p] = s["z"][p]
        y = jnp.concatenate(s["y_rows"], axis=0)
        mu = head_sum(y) * (1.0 / H)
        d = y - mu
        var = head_sum(d * d) * (1.0 / H)
        yn = d * lax.rsqrt(var + GN_EPS) * lg_ref[...] + lb_ref[...] + s["bonus"]
        o_ref[bi] = (yn * _silu(ga_ref[bi])).astype(o_ref.dtype)
        s.clear()

    prep = [prep_lora, prep_decay, prep_rows]
    middle = [m_blocks, m_neumann2, m_neumann3]
    for sz in (8, 16, 32):
        middle += [m_merge_a(sz), m_merge_b(sz)]
    middle += [m_x1, m_wu, m_qymn]
    finish = [tail(cc) for cc in range(n_chunks)] + [epilogue]

    for f in prep:
        f(0)
    for bi in range(nb):
        side = []
        if bi > 0:
            side += [(g, bi - 1) for g in finish]
        if bi + 1 < nb:
            side += [(g, bi + 1) for g in prep]
        every = max(1, len(middle) // (len(side) + 1)) if side else 0
        for i, f in enumerate(middle):
            f(bi)
            if side and (i + 1) % every == 0:
                g, other = side.pop(0)
                g(other)
        for g, other in side:
            g(other)
    for g in finish:
        g(nb - 1)


def _rwkv_pipelined(rw, ga, w0, w2, a0, a2, k_k, k_a, r_k, lnx_g, lnx_b, nb, rb):
    b, s, _ = rw.shape
    blk3 = lambda i, j: (i, j, 0)
    const = lambda i, j: (0, 0)
    vec = pl.BlockSpec((1, RW_WIDTH), const)
    lora = pl.BlockSpec((DECAY_LORA, RW_WIDTH), const)
    return pl.pallas_call(
        _rwkv_pipelined_kernel,
        grid=(b // nb, s // rb),
        in_specs=[pl.BlockSpec((nb, rb, SHIFT_WIDTH), blk3),
                  pl.BlockSpec((nb, rb, RW_WIDTH), blk3),
                  vec, lora, vec, lora, vec, vec, vec, vec, vec],
        out_specs=pl.BlockSpec((nb, rb, RW_WIDTH), blk3),
        out_shape=jax.ShapeDtypeStruct((b, s, RW_WIDTH), BF16),
        scratch_shapes=[pltpu.VMEM((nb, RW_WIDTH // PAIR, PAIR, PAIR), F32)],
        compiler_params=pltpu.CompilerParams(dimension_semantics=("parallel", "arbitrary"),
                                             vmem_limit_bytes=VMEM_LIMIT),
        name="rwkv7_mix",
    )(rw, ga, w0, w2, a0, a2, k_k, k_a, r_k, lnx_g, lnx_b)


SLAB = 4 * AT_HEAD_DIM
KEY_WIN = (LEFT_CHUNKS + 2) * CHUNK
N_BIAS = LEFT_CHUNKS + 2


def _attn_kernel(q_ref, k_ref, v_ref, gb_ref, bias_ref, o_ref):
    cb = q_ref.shape[0] // CHUNK
    j = pl.program_id(1)
    lane_head = lax.broadcasted_iota(jnp.int32, (CHUNK, SLAB), 1) // AT_HEAD_DIM
    head_masks = [lane_head == h for h in range(SLAB // AT_HEAD_DIM)]
    for cc in range(cb):
        c = j * cb + cc
        start = pl.multiple_of(jnp.maximum(c - (LEFT_CHUNKS + 1), 0) * CHUNK, CHUNK)
        table = jnp.minimum(c, N_BIAS - 1)
        rows = slice(cc * CHUNK, (cc + 1) * CHUNK)
        for s0 in range(0, AT_WIDTH, SLAB):
            cols = slice(s0, s0 + SLAB)
            qs = q_ref[rows, cols]
            q_bd = jnp.concatenate([jnp.where(mh, qs, jnp.zeros_like(qs)) for mh in head_masks], axis=0)
            kb = k_ref[pl.ds(start, KEY_WIN), cols]
            s = lax.dot_general(q_bd, kb, (((1,), (1,)), ((), ())), preferred_element_type=F32)
            s = s + bias_ref[table, s0:s0 + SLAB, :]
            mx = jnp.max(s, axis=1, keepdims=True)
            e = jnp.exp2(s - mx)
            l = jnp.sum(e, axis=1, keepdims=True)
            vb = v_ref[pl.ds(start, KEY_WIN), cols]
            o_bd = jnp.dot(e.astype(BF16), vb, preferred_element_type=F32) * (1.0 / l)
            o = jnp.zeros((CHUNK, SLAB), F32)
            for h, mh in enumerate(head_masks):
                o = jnp.where(mh, o_bd[h * CHUNK:(h + 1) * CHUNK, :], o)
            o_ref[rows, cols] = (o * _silu(gb_ref[rows, cols])).astype(o_ref.dtype)


def _attn(q, k, v, gb, bias_tables, cb):
    b, s, _ = q.shape
    qblk = lambda i, j: (i, j, 0)
    full = lambda i, j: (i, 0, 0)
    return pl.pallas_call(
        _attn_kernel,
        grid=(b, s // (cb * CHUNK)),
        in_specs=[pl.BlockSpec((None, cb * CHUNK, AT_WIDTH), qblk),
                  pl.BlockSpec((None, s, AT_WIDTH), full),
                  pl.BlockSpec((None, s, AT_WIDTH), full),
                  pl.BlockSpec((None, cb * CHUNK, AT_WIDTH), qblk),
                  pl.BlockSpec(bias_tables.shape, lambda i, j: (0, 0, 0))],
        out_specs=pl.BlockSpec((None, cb * CHUNK, AT_WIDTH), qblk),
        out_shape=jax.ShapeDtypeStruct((b, s, AT_WIDTH), BF16),
        compiler_params=pltpu.CompilerParams(dimension_semantics=("parallel", "arbitrary"),
                                             vmem_limit_bytes=VMEM_LIMIT),
        name="chunk_attention",
    )(q, k, v, gb, bias_tables)


GMLP_SUB = 256


def _layer1_kernel(x_ref, ya_ref, yb_ref, woe_ref, g1_ref, wio_ref, lng_ref, lnb_ref, sgw_ref,
                   sgb_ref, woo_ref, fg_ref, o_ref):
    tm = x_ref.shape[0]
    gd = SG_WIDTH // SG_GROUPS
    pr = lax.broadcasted_iota(jnp.int32, (SG_CHUNK, SG_CHUNK), 0) // CHUNK
    pc = lax.broadcasted_iota(jnp.int32, (SG_CHUNK, SG_CHUNK), 1) // CHUNK
    causal = pc <= pr
    wg = [jnp.where(causal, sgw_ref[g], 0.0).astype(BF16) for g in range(SG_GROUPS)]

    subs = [slice(i * GMLP_SUB, (i + 1) * GMLP_SUB) for i in range(tm // GMLP_SUB)]
    h1 = [x_ref[rs, :] + jnp.dot(ya_ref[rs, :], woe_ref[0:RW_WIDTH, :], preferred_element_type=F32)
          + jnp.dot(yb_ref[rs, :], woe_ref[RW_WIDTH:, :], preferred_element_type=F32) for rs in subs]
    n1 = [_rmsnorm(h, g1_ref[...]).astype(BF16) for h in h1]
    vv = [_gelu_tanh(jnp.dot(n, wio_ref[:, SG_WIDTH:2 * SG_WIDTH], preferred_element_type=F32)) for n in n1]
    u = [_gelu_tanh(jnp.dot(n, wio_ref[:, 0:SG_WIDTH], preferred_element_type=F32)) for n in n1]
    gate = [jnp.dot(n, wio_ref[:, 2 * SG_WIDTH:], preferred_element_type=F32) for n in n1]
    vln = []
    for x in vv:
        mu = jnp.mean(x, axis=-1, keepdims=True)
        dv = x - mu
        var = jnp.mean(dv * dv, axis=-1, keepdims=True)
        vln.append((dv * lax.rsqrt(var + LN_EPS) * lng_ref[...] + lnb_ref[...]).astype(BF16))
    sv = []
    for xl in vln:
        cols = [jnp.concatenate([jnp.dot(wg[g], xl[nb * SG_CHUNK:(nb + 1) * SG_CHUNK, g * gd:(g + 1) * gd],
                                         preferred_element_type=F32) + sgb_ref[:, g * gd:(g + 1) * gd]
                                 for nb in range(GMLP_SUB // SG_CHUNK)], axis=0) for g in range(SG_GROUPS)]
        sv.append(jnp.concatenate(cols, axis=1))
    y = [((ui * svi) * _silu(gi)).astype(BF16) for ui, svi, gi in zip(u, sv, gate)]
    for rs, h, yi in zip(subs, h1, y):
        h2 = h + jnp.dot(yi, woo_ref[...], preferred_element_type=F32)
        o_ref[rs, :] = _rmsnorm(h2, fg_ref[...])


def _layer1(x2d, ya, yb, woe, g1, wio, lng, lnb, sgw, sgb_full, woo, fg, tm):
    t = x2d.shape[0]
    row = lambda i: (i, 0)
    const = lambda i: (0, 0)
    vec = pl.BlockSpec((1, D_MODEL), const)
    return pl.pallas_call(
        _layer1_kernel,
        grid=(t // tm,),
        in_specs=[pl.BlockSpec((tm, D_MODEL), row),
                  pl.BlockSpec((tm, RW_WIDTH), row),
                  pl.BlockSpec((tm, AT_WIDTH), row),
                  pl.BlockSpec((D_MODEL, D_MODEL), const),
                  vec,
                  pl.BlockSpec((D_MODEL, 3 * SG_WIDTH), const),
                  vec, vec,
                  pl.BlockSpec((SG_GROUPS, SG_CHUNK, SG_CHUNK), lambda i: (0, 0, 0)),
                  pl.BlockSpec((SG_CHUNK, SG_WIDTH), const),
                  pl.BlockSpec((SG_WIDTH, D_MODEL), const),
                  vec],
        out_specs=pl.BlockSpec((tm, D_MODEL), row),
        out_shape=jax.ShapeDtypeStruct((t, D_MODEL), F32),
        compiler_params=pltpu.CompilerParams(dimension_semantics=("parallel",),
                                             vmem_limit_bytes=VMEM_LIMIT),
        name="gmlp_layer",
    )(x2d, ya, yb, woe, g1, wio, lng, lnb, sgw, sgb_full, woo, fg)


def _bias_tables(att_bias):
    shift = np.array([c * CHUNK for c in range(N_BIAS - 1)] + [(LEFT_CHUNKS + 1) * CHUNK])
    n = np.arange(-(CHUNK - 1), KEY_WIN)
    idx = np.clip(shift[:, None] - n[None, :], -REL_CLIP, REL_CLIP) + REL_CLIP
    diag = jnp.transpose(att_bias[:, idx], (1, 0, 2)).astype(F32) * LOG2E
    diag = jnp.pad(diag, ((0, 0), (0, 0), (0, DIAG_PAD - diag.shape[-1])))
    n_heads = att_bias.shape[0]
    return pl.pallas_call(
        _bias_unfold_kernel,
        grid=(N_BIAS,),
        in_specs=[pl.BlockSpec((None, n_heads, DIAG_PAD), lambda c: (c, 0, 0))],
        out_specs=pl.BlockSpec((None, n_heads * CHUNK, KEY_WIN), lambda c: (c, 0, 0)),
        out_shape=jax.ShapeDtypeStruct((N_BIAS, n_heads * CHUNK, KEY_WIN), F32),
        compiler_params=pltpu.CompilerParams(dimension_semantics=("parallel",)),
        name="bias_unfold",
    )(diag)


DIAG_PAD = 768


def _bias_unfold_kernel(diag_ref, o_ref):
    c = pl.program_id(0)
    kj = lax.broadcasted_iota(jnp.int32, (CHUNK, KEY_WIN), 1)
    late = c == N_BIAS - 1
    lo = jnp.where(late, CHUNK, 0)
    hi = jnp.where(late, KEY_WIN, (c + 1) * CHUNK)
    valid = jnp.logical_and(kj >= lo, kj < hi)
    for h in range(diag_ref.shape[0]):
        rows = jnp.broadcast_to(diag_ref[h:h + 1, :], (CHUNK, DIAG_PAD))
        unfolded = pltpu.roll(rows, DIAG_PAD - (CHUNK - 1), 1, stride=1, stride_axis=0)
        o_ref[h * CHUNK:(h + 1) * CHUNK, :] = jnp.where(valid, unfolded[:, :KEY_WIN], NEG_INF)


def kernel(x, norm_g, w_in_e, shift_mu, rw_w0, rw_w2, rw_a0, rw_a2, rw_kk, rw_ka, rw_rk, rw_lnx_g,
           rw_lnx_b, att_bias, w_out_e, w_in_o, sg_ln_g, sg_ln_b, sg_w, sg_b, w_out_o, final_g):
    b, s, d = x.shape
    assert d == D_MODEL and s % 512 == 0 and s >= KEY_WIN and b % RWKV_SEQS_PER_STEP == 0
    x2d = x.reshape(b * s, d)
    row = lambda a: a.reshape(1, -1).astype(F32)

    rw, ga, q, k, v, gb = _in_proj_even(x2d, row(norm_g[0]), w_in_e[0].astype(BF16), row(shift_mu[0]),
                                        seq_len=s, tm=512)
    r3 = lambda a: a.reshape(b, s, a.shape[-1])
    ya = _rwkv_pipelined(r3(rw), r3(ga), row(rw_w0[0]), rw_w2[0], row(rw_a0[0]), rw_a2[0], row(rw_kk[0]),
                         row(rw_ka[0]), row(rw_rk[0]), row(rw_lnx_g[0]), row(rw_lnx_b[0]),
                         nb=RWKV_SEQS_PER_STEP, rb=256)
    yb = _attn(r3(q), r3(k), r3(v), r3(gb), _bias_tables(att_bias[0]), cb=16)

    sgb_full = jnp.repeat(sg_b[0].T, SG_WIDTH // SG_GROUPS, axis=1).astype(F32)
    out = _layer1(x2d, ya.reshape(b * s, -1), yb.reshape(b * s, -1), w_out_e[0].astype(BF16),
                  row(norm_g[1]), w_in_o[0].astype(BF16), row(sg_ln_g[0]), row(sg_ln_b[0]), sg_w[0],
                  sgb_full, w_out_o[0].astype(BF16), row(final_g), tm=512)
    return out.reshape(b, s, d)
```

```python
import functools
import math

import jax
import jax.numpy as jnp
import numpy as np
from jax import lax
from jax.experimental import pallas as pl
from jax.experimental.pallas import tpu as pltpu

F32 = jnp.float32
BF16 = jnp.bfloat16

D_MODEL = 1024
CHUNK = 64
RW_HEAD_DIM = 64
RW_WIDTH = 512
DECAY_LORA = 64
AAA_LORA = 64
AT_HEAD_DIM = 64
AT_WIDTH = 512
LEFT_CHUNKS = 8
REL_CLIP = 2 * CHUNK
SG_CHUNK = 128
SG_WIDTH = 1024
SG_GROUPS = 8
SHIFT_WIDTH = 3 * RW_WIDTH + DECAY_LORA + AAA_LORA
EVEN_IN = SHIFT_WIDTH + RW_WIDTH + 4 * AT_WIDTH
RMS_EPS = 1e-6
LN_EPS = 1e-5
GN_EPS = 64e-5
NEG_INF = -1e30
LOG2E = math.log2(math.e)

VMEM_LIMIT = 56 * 1024 * 1024

PAIR = 2 * RW_HEAD_DIM


def _dot(a, b):
    return jnp.dot(a.astype(BF16), b.astype(BF16), preferred_element_type=F32)


def _dot_nt(a, b):
    return lax.dot_general(a.astype(BF16), b.astype(BF16), (((1,), (1,)), ((), ())),
                           preferred_element_type=F32)


def _split(x):
    hi = x.astype(BF16)
    lo = (x - hi.astype(F32)).astype(BF16)
    return hi, lo


def _sigmoid(x):
    return 1.0 / (1.0 + jnp.exp(-x))


def _silu(x):
    hx = 0.5 * x
    return hx + hx * jnp.tanh(hx)


def _gelu_tanh(x):
    c = math.sqrt(2.0 / math.pi)
    hx = 0.5 * x
    return hx + hx * jnp.tanh(x * (c + (c * 0.044715) * (x * x)))


def _rmsnorm(x, g):
    ms = jnp.mean(x * x, axis=-1, keepdims=True)
    return x * lax.rsqrt(ms + RMS_EPS) * g


def _col_chunks(lo, hi, width=256):
    out = []
    c = lo
    while c < hi:
        w = min(width, hi - c)
        out.append((c, w))
        c += w
    return out


def _in_proj_even_kernel(x_ref, g_ref, w_ref, mu_ref, rw_ref, ga_ref, q_ref, k_ref, v_ref, gb_ref,
                         n_ref, carry_ref, *, tiles_per_seq):
    tm = x_ref.shape[0]
    i = pl.program_id(0)
    n_ref[...] = _rmsnorm(x_ref[...], g_ref[...]).astype(BF16)

    @pl.when(i % tiles_per_seq == 0)
    def _():
        carry_ref[...] = jnp.zeros_like(carry_ref)

    for c0, cw in _col_chunks(0, SHIFT_WIDTH):
        pc = jnp.dot(n_ref[...], w_ref[:, c0:c0 + cw], preferred_element_type=F32)
        row0 = lax.broadcasted_iota(jnp.int32, (tm, cw), 0) == 0
        prev = jnp.where(row0, carry_ref[0:1, c0:c0 + cw], pltpu.roll(pc, 1, 0))
        carry_ref[0:1, c0:c0 + cw] = pc[tm - 1:tm, :]
        rw_ref[:, c0:c0 + cw] = pc + (prev - pc) * mu_ref[:, c0:c0 + cw]

    base = SHIFT_WIDTH
    for dst, scale in ((ga_ref, None), (q_ref, LOG2E / math.sqrt(AT_HEAD_DIM)), (k_ref, None),
                       (v_ref, None), (gb_ref, None)):
        for c0, cw in _col_chunks(0, RW_WIDTH):
            pc = jnp.dot(n_ref[...], w_ref[:, base + c0:base + c0 + cw], preferred_element_type=F32)
            if scale is not None:
                pc = pc * scale
            dst[:, c0:c0 + cw] = pc.astype(dst.dtype)
        base += RW_WIDTH


def _in_proj_even(x2d, g, w_bf16, mu, seq_len, tm):
    t = x2d.shape[0]
    row = lambda i: (i, 0)
    const = lambda i: (0, 0)
    kern = functools.partial(_in_proj_even_kernel, tiles_per_seq=seq_len // tm)
    return pl.pallas_call(
        kern,
        grid=(t // tm,),
        in_specs=[pl.BlockSpec((tm, D_MODEL), row),
                  pl.BlockSpec((1, D_MODEL), const),
                  pl.BlockSpec((D_MODEL, EVEN_IN), const),
                  pl.BlockSpec((1, SHIFT_WIDTH), const)],
        out_specs=[pl.BlockSpec((tm, SHIFT_WIDTH), row),
                   pl.BlockSpec((tm, RW_WIDTH), row),
                   pl.BlockSpec((tm, AT_WIDTH), row),
                   pl.BlockSpec((tm, AT_WIDTH), row),
                   pl.BlockSpec((tm, AT_WIDTH), row),
                   pl.BlockSpec((tm, AT_WIDTH), row)],
        out_shape=[jax.ShapeDtypeStruct((t, SHIFT_WIDTH), F32),
                   jax.ShapeDtypeStruct((t, RW_WIDTH), F32),
                   jax.ShapeDtypeStruct((t, AT_WIDTH), BF16),
                   jax.ShapeDtypeStruct((t, AT_WIDTH), BF16),
                   jax.ShapeDtypeStruct((t, AT_WIDTH), BF16),
                   jax.ShapeDtypeStruct((t, AT_WIDTH), F32)],
        scratch_shapes=[pltpu.VMEM((tm, D_MODEL), BF16),
                        pltpu.VMEM((8, SHIFT_WIDTH), F32)],
        compiler_params=pltpu.CompilerParams(dimension_semantics=("arbitrary",),
                                             vmem_limit_bytes=VMEM_LIMIT),
        name="in_proj_even",
    )(x2d, g, w_bf16, mu)


EXP_NEG_HALF = math.exp(-0.5)
RWKV_SEQS_PER_STEP = 2


def _rwkv_pipelined_kernel(rw_ref, ga_ref, w0_ref, w2_ref, a0_ref, a2_ref, kk_ref, ka_ref, rk_ref,
                           lg_ref, lb_ref, o_ref, z_ref):
    nb, rb = rw_ref.shape[0], rw_ref.shape[1]
    n_chunks = rb // CHUNK
    n_pairs = RW_WIDTH // PAIR
    H = RW_HEAD_DIM

    @pl.when(pl.program_id(1) == 0)
    def _():
        z_ref[...] = jnp.zeros_like(z_ref)

    ri = lax.broadcasted_iota(jnp.int32, (PAIR, PAIR), 0)
    ci = lax.broadcasted_iota(jnp.int32, (PAIR, PAIR), 1)
    same_head = (ri // H) == (ci // H)
    tril_bd = jnp.logical_and(same_head, ci <= ri)
    stril_bd = jnp.logical_and(same_head, ci < ri)
    eye_mask = ri == ci
    eye = jnp.where(eye_mask, 1.0, 0.0).astype(F32)
    blk = {sz: (ri // sz) == (ci // sz) for sz in (8, 16, 32, 64)}
    merge = {sz: jnp.logical_and(blk[2 * sz], jnp.logical_not(blk[sz])) for sz in (8, 16, 32)}
    tr = lax.broadcasted_iota(jnp.int32, (rb, rb), 0)
    tc = lax.broadcasted_iota(jnp.int32, (rb, rb), 1)
    chunk_tril = jnp.where(jnp.logical_and((tr // CHUNK) == (tc // CHUNK), tc <= tr), 1.0, 0.0).astype(BF16)
    lane_head = lax.broadcasted_iota(jnp.int32, (CHUNK, PAIR), 1) // H
    head0 = lane_head == 0
    head1 = lane_head == 1
    first_head = lax.broadcasted_iota(jnp.int32, (rb, PAIR), 1) < H
    zeros = jnp.zeros((PAIR, PAIR), F32)

    def head_sum(x):
        out = []
        for p in range(n_pairs):
            xs = x[:, p * PAIR:(p + 1) * PAIR]
            s0 = jnp.sum(jnp.where(first_head, xs, 0.0), axis=1, keepdims=True)
            s1 = jnp.sum(jnp.where(first_head, 0.0, xs), axis=1, keepdims=True)
            out.append(jnp.where(first_head, s0, s1))
        return jnp.concatenate(out, axis=1)

    def bd(x, cc, p):
        xs = x[cc * CHUNK:(cc + 1) * CHUNK, p * PAIR:(p + 1) * PAIR]
        zero = jnp.zeros_like(xs)
        return jnp.concatenate([jnp.where(head0, xs, zero), jnp.where(head1, xs, zero)], axis=0)

    def low_rows(x, sz):
        return jnp.concatenate([x[s:s + sz] for s in range(sz, PAIR, 2 * sz)], axis=0)

    def spread_low_rows(y, sz):
        zero = jnp.zeros((sz, y.shape[1]), y.dtype)
        parts = []
        for i in range(PAIR // (2 * sz)):
            parts += [zero, y[i * sz:(i + 1) * sz]]
        return jnp.concatenate(parts, axis=0)

    def blocks(top, bottom, mask):
        return jnp.where(mask, jnp.concatenate([top, bottom], axis=0), 0.0)

    chains = [(cc, p) for cc in range(n_chunks) for p in range(n_pairs)]
    st = [dict() for _ in range(nb)]

    def prep_lora(bi):
        s = st[bi]
        wd = rw_ref[bi, :, 3 * RW_WIDTH:3 * RW_WIDTH + DECAY_LORA]
        ad = rw_ref[bi, :, 3 * RW_WIDTH + DECAY_LORA:SHIFT_WIDTH]
        s["zw"] = w0_ref[...] + _dot(jnp.tanh(wd), w2_ref[...])
        s["za"] = a0_ref[...] + _dot(ad, a2_ref[...])

    def prep_decay(bi):
        s = st[bi]
        lw = -EXP_NEG_HALF * _sigmoid(s["zw"])
        lw_hi, lw_lo = _split(lw)
        s["lw"] = lw
        s["c"] = _dot(chunk_tril, lw_hi) + _dot(chunk_tril, lw_lo)

    def prep_rows(bi):
        s = st[bi]
        r = rw_ref[bi, :, 0:RW_WIDTH]
        k = rw_ref[bi, :, RW_WIDTH:2 * RW_WIDTH]
        v = rw_ref[bi, :, 2 * RW_WIDTH:3 * RW_WIDTH]
        c, lw = s["c"], s["lw"]
        c_last = jnp.concatenate(
            [jnp.broadcast_to(c[(cc + 1) * CHUNK - 1:(cc + 1) * CHUNK, :], (CHUNK, RW_WIDTH))
             for cc in range(n_chunks)], axis=0)
        e_c = jnp.exp(c)
        e_nc = jnp.exp(-c)
        e_cm = jnp.exp(c - lw)
        e_lc = jnp.exp(c_last - c)
        g_l = jnp.exp(c_last)
        a_g = _sigmoid(s["za"])
        kk = k * kk_ref[...]
        ss = head_sum(kk * kk)
        kkn = kk * lax.rsqrt(jnp.maximum(ss, 1e-24))
        k_m = k * (1.0 + (a_g - 1.0) * ka_ref[...])
        b_v = kkn * a_g
        rh = r * e_c
        kh = k_m * e_nc
        bh = b_v * e_nc
        ah16 = (-kkn * e_cm).astype(BF16)
        bt16 = (b_v * e_lc).astype(BF16)
        kt16 = (k_m * e_lc).astype(BF16)
        v16 = v.astype(BF16)
        s["bonus"] = head_sum(r * k_m * rk_ref[...]) * v
        for key in ("rh_bd", "ah_bd", "v_bd", "bt_t", "kt_t", "g_last", "g"):
            s[key] = {}
        for ch in chains:
            cc, p = ch
            rows = slice(cc * CHUNK, (cc + 1) * CHUNK)
            sl = slice(p * PAIR, (p + 1) * PAIR)
            s["rh_bd"][ch] = bd(rh, cc, p)
            s["ah_bd"][ch] = bd(ah16, cc, p)
            s["v_bd"][ch] = bd(v16, cc, p)
            s["bt_t"][ch] = bd(bt16, cc, p).T
            s["kt_t"][ch] = bd(kt16, cc, p).T
            s["g_last"][ch] = g_l[cc * CHUNK:cc * CHUNK + 1, sl]
            lhs = jnp.concatenate([s["rh_bd"][ch].astype(BF16), s["ah_bd"][ch]], axis=0)
            rhs = jnp.concatenate([kh[rows, sl], bh[rows, sl]], axis=0)
            s["g"][ch] = _dot_nt(lhs, rhs)

    def m_blocks(bi):
        s = st[bi]
        g = s.pop("g")
        gs = {ch: pltpu.roll(g[ch], H, 1) for ch in chains}
        s["a_rk"] = {ch: blocks(g[ch][0:H], gs[ch][H:2 * H], tril_bd) for ch in chains}
        s["a_rb"] = {ch: blocks(gs[ch][0:H], g[ch][H:2 * H], tril_bd) for ch in chains}
        s["a_ak"] = {ch: blocks(g[ch][2 * H:3 * H], gs[ch][3 * H:4 * H], stril_bd) for ch in chains}
        s["a_ab"] = {ch: blocks(gs[ch][2 * H:3 * H], g[ch][3 * H:4 * H], stril_bd) for ch in chains}
        s["a8"] = {ch: jnp.where(blk[8], s["a_ab"][ch], 0.0) for ch in chains}
        s["a2"] = {ch: _dot(s["a8"][ch], s["a8"][ch]) for ch in chains}

    def m_neumann2(bi):
        s = st[bi]
        b1 = {ch: eye + s["a8"][ch] for ch in chains}
        s["b2"] = {ch: b1[ch] + _dot(s["a2"][ch], b1[ch]) for ch in chains}
        s["a4"] = {ch: _dot(s["a2"][ch], s["a2"][ch]) for ch in chains}

    def m_neumann3(bi):
        s = st[bi]
        s["t"] = {ch: s["b2"][ch] + _dot(s["a4"][ch], s["b2"][ch]) for ch in chains}

    def m_merge_a(sz):
        def f(bi):
            s = st[bi]
            s["ed"] = {ch: _dot(low_rows(jnp.where(merge[sz], s["a_ab"][ch], 0.0), sz), s["t"][ch])
                       for ch in chains}
        return f

    def m_merge_b(sz):
        def f(bi):
            s = st[bi]
            upd = {ch: _dot(low_rows(s["t"][ch], sz), spread_low_rows(s["ed"][ch], sz)) for ch in chains}
            s["t"] = {ch: s["t"][ch] + spread_low_rows(upd[ch], sz) for ch in chains}
        return f

    def m_x1(bi):
        s = st[bi]
        s["x1"] = {ch: _dot(s["a_ak"][ch], s["v_bd"][ch]) for ch in chains}

    def m_wu(bi):
        s = st[bi]
        s["wu"] = {ch: _dot(s["t"][ch], jnp.concatenate([s["ah_bd"][ch], s["x1"][ch].astype(BF16)], axis=1))
                   for ch in chains}

    def m_qymn(bi):
        s = st[bi]
        s["qymn"] = {}
        for ch in chains:
            lhs = jnp.concatenate([jnp.concatenate([s["a_rb"][ch], s["a_rk"][ch]], axis=1).astype(BF16),
                                   jnp.concatenate([s["bt_t"][ch], s["kt_t"][ch]], axis=1)], axis=0)
            rhs = jnp.concatenate([s["wu"][ch].astype(BF16),
                                   jnp.concatenate([zeros.astype(BF16), s["v_bd"][ch]], axis=1)], axis=0)
            s["qymn"][ch] = _dot(lhs, rhs)
        s["z"] = [z_ref[bi, p] for p in range(n_pairs)]
        s["y_rows"] = []

    def tail(cc):
        def f(bi):
            s = st[bi]
            y_pairs = []
            for p in range(n_pairs):
                ch = (cc, p)
                qymn = s["qymn"][ch]
                q = s["rh_bd"][ch] + qymn[0:PAIR, 0:PAIR]
                m = qymn[PAIR:2 * PAIR, 0:PAIR] + jnp.where(eye_mask, s["g_last"][ch], 0.0)
                zy = _dot(jnp.concatenate([m, q], axis=0), s["z"][p])
                s["z"][p] = zy[0:PAIR] + qymn[PAIR:2 * PAIR, PAIR:2 * PAIR]
                y_bd = zy[PAIR:2 * PAIR] + qymn[0:PAIR, PAIR:2 * PAIR]
                y_pairs.append(y_bd[0:CHUNK] + y_bd[CHUNK:PAIR])
            s["y_rows"].append(jnp.concatenate(y_pairs, axis=1))
        return f

    def epilogue(bi):
        s = st[bi]
        for p in range(n_pairs):
            z_ref[bi, p] = s["z"][p]
        y = jnp.concatenate(s["y_rows"], axis=0)
        mu = head_sum(y) * (1.0 / H)
        d = y - mu
        var = head_sum(d * d) * (1.0 / H)
        yn = d * lax.rsqrt(var + GN_EPS) * lg_ref[...] + lb_ref[...] + s["bonus"]
        o_ref[bi] = (yn * _silu(ga_ref[bi])).astype(o_ref.dtype)
        s.clear()

    prep = [prep_lora, prep_decay, prep_rows]
    middle = [m_blocks, m_neumann2, m_neumann3]
    for sz in (8, 16, 32):
        middle += [m_merge_a(sz), m_merge_b(sz)]
    middle += [m_x1, m_wu, m_qymn]
    finish = [tail(cc) for cc in range(n_chunks)] + [epilogue]

    for f in prep:
        f(0)
    for bi in range(nb):
        side = []
        if bi > 0:
            side += [(g, bi - 1) for g in finish]
        if bi + 1 < nb:
            side += [(g, bi + 1) for g in prep]
        every = max(1, len(middle) // (len(side) + 1)) if side else 0
        for i, f in enumerate(middle):
            f(bi)
            if side and (i + 1) % every == 0:
                g, other = side.pop(0)
                g(other)
        for g, other in side:
            g(other)
    for g in finish:
        g(nb - 1)


def _rwkv_pipelined(rw, ga, w0, w2, a0, a2, k_k, k_a, r_k, lnx_g, lnx_b, nb, rb):
    b, s, _ = rw.shape
    blk3 = lambda i, j: (i, j, 0)
    const = lambda i, j: (0, 0)
    vec = pl.BlockSpec((1, RW_WIDTH), const)
    lora = pl.BlockSpec((DECAY_LORA, RW_WIDTH), const)
    return pl.pallas_call(
        _rwkv_pipelined_kernel,
        grid=(b // nb, s // rb),
        in_specs=[pl.BlockSpec((nb, rb, SHIFT_WIDTH), blk3),
                  pl.BlockSpec((nb, rb, RW_WIDTH), blk3),
                  vec, lora, vec, lora, vec, vec, vec, vec, vec],
        out_specs=pl.BlockSpec((nb, rb, RW_WIDTH), blk3),
        out_shape=jax.ShapeDtypeStruct((b, s, RW_WIDTH), BF16),
        scratch_shapes=[pltpu.VMEM((nb, RW_WIDTH // PAIR, PAIR, PAIR), F32)],
        compiler_params=pltpu.CompilerParams(dimension_semantics=("parallel", "arbitrary"),
                                             vmem_limit_bytes=VMEM_LIMIT),
        name="rwkv7_mix",
    )(rw, ga, w0, w2, a0, a2, k_k, k_a, r_k, lnx_g, lnx_b)


SLAB = 4 * AT_HEAD_DIM
KEY_WIN = (LEFT_CHUNKS + 2) * CHUNK
N_BIAS = LEFT_CHUNKS + 2


def _attn_kernel(q_ref, k_ref, v_ref, gb_ref, bias_ref, o_ref):
    cb = q_ref.shape[0] // CHUNK
    j = pl.program_id(1)
    lane_head = lax.broadcasted_iota(jnp.int32, (CHUNK, SLAB), 1) // AT_HEAD_DIM
    head_masks = [lane_head == h for h in range(SLAB // AT_HEAD_DIM)]
    for cc in range(cb):
        c = j * cb + cc
        start = pl.multiple_of(jnp.maximum(c - (LEFT_CHUNKS + 1), 0) * CHUNK, CHUNK)
        table = jnp.minimum(c, N_BIAS - 1)
        rows = slice(cc * CHUNK, (cc + 1) * CHUNK)
        for s0 in range(0, AT_WIDTH, SLAB):
            cols = slice(s0, s0 + SLAB)
            qs = q_ref[rows, cols]
            q_bd = jnp.concatenate([jnp.where(mh, qs, jnp.zeros_like(qs)) for mh in head_masks], axis=0)
            kb = k_ref[pl.ds(start, KEY_WIN), cols]
            s = lax.dot_general(q_bd, kb, (((1,), (1,)), ((), ())), preferred_element_type=F32)
            s = s + bias_ref[table, s0:s0 + SLAB, :]
            mx = jnp.max(s, axis=1, keepdims=True)
            e = jnp.exp2(s - mx)
            l = jnp.sum(e, axis=1, keepdims=True)
            vb = v_ref[pl.ds(start, KEY_WIN), cols]
            o_bd = jnp.dot(e.astype(BF16), vb, preferred_element_type=F32) * (1.0 / l)
            o = jnp.zeros((CHUNK, SLAB), F32)
            for h, mh in enumerate(head_masks):
                o = jnp.where(mh, o_bd[h * CHUNK:(h + 1) * CHUNK, :], o)
            o_ref[rows, cols] = (o * _silu(gb_ref[rows, cols])).astype(o_ref.dtype)


def _attn(q, k, v, gb, bias_tables, cb):
    b, s, _ = q.shape
    qblk = lambda i, j: (i, j, 0)
    full = lambda i, j: (i, 0, 0)
    return pl.pallas_call(
        _attn_kernel,
        grid=(b, s // (cb * CHUNK)),
        in_specs=[pl.BlockSpec((None, cb * CHUNK, AT_WIDTH), qblk),
                  pl.BlockSpec((None, s, AT_WIDTH), full),
                  pl.BlockSpec((None, s, AT_WIDTH), full),
                  pl.BlockSpec((None, cb * CHUNK, AT_WIDTH), qblk),
                  pl.BlockSpec(bias_tables.shape, lambda i, j: (0, 0, 0))],
        out_specs=pl.BlockSpec((None, cb * CHUNK, AT_WIDTH), qblk),
        out_shape=jax.ShapeDtypeStruct((b, s, AT_WIDTH), BF16),
        compiler_params=pltpu.CompilerParams(dimension_semantics=("parallel", "arbitrary"),
                                             vmem_limit_bytes=VMEM_LIMIT),
        name="chunk_attention",
    )(q, k, v, gb, bias_tables)


GMLP_SUB = 256


def _layer1_kernel(x_ref, ya_ref, yb_ref, woe_ref, g1_ref, wio_ref, lng_ref, lnb_ref, sgw_ref,
                   sgb_ref, woo_ref, fg_ref, o_ref):
    tm = x_ref.shape[0]
    gd = SG_WIDTH // SG_GROUPS
    pr = lax.broadcasted_iota(jnp.int32, (SG_CHUNK, SG_CHUNK), 0) // CHUNK
    pc = lax.broadcasted_iota(jnp.int32, (SG_CHUNK, SG_CHUNK), 1) // CHUNK
    causal = pc <= pr
    wg = [jnp.where(causal, sgw_ref[g], 0.0).astype(BF16) for g in range(SG_GROUPS)]

    subs = [slice(i * GMLP_SUB, (i + 1) * GMLP_SUB) for i in range(tm // GMLP_SUB)]
    h1 = [x_ref[rs, :] + jnp.dot(ya_ref[rs, :], woe_ref[0:RW_WIDTH, :], preferred_element_type=F32)
          + jnp.dot(yb_ref[rs, :], woe_ref[RW_WIDTH:, :], preferred_element_type=F32) for rs in subs]
    n1 = [_rmsnorm(h, g1_ref[...]).astype(BF16) for h in h1]
    vv = [_gelu_tanh(jnp.dot(n, wio_ref[:, SG_WIDTH:2 * SG_WIDTH], preferred_element_type=F32)) for n in n1]
    u = [_gelu_tanh(jnp.dot(n, wio_ref[:, 0:SG_WIDTH], preferred_element_type=F32)) for n in n1]
    gate = [jnp.dot(n, wio_ref[:, 2 * SG_WIDTH:], preferred_element_type=F32) for n in n1]
    vln = []
    for x in vv:
        mu = jnp.mean(x, axis=-1, keepdims=True)
        dv = x - mu
        var = jnp.mean(dv * dv, axis=-1, keepdims=True)
        vln.append((dv * lax.rsqrt(var + LN_EPS) * lng_ref[...] + lnb_ref[...]).astype(BF16))
    sv = []
    for xl in vln:
        cols = [jnp.concatenate([jnp.dot(wg[g], xl[nb * SG_CHUNK:(nb + 1) * SG_CHUNK, g * gd:(g + 1) * gd],
                                         preferred_element_type=F32) + sgb_ref[:, g * gd:(g + 1) * gd]
                                 for nb in range(GMLP_SUB // SG_CHUNK)], axis=0) for g in range(SG_GROUPS)]
        sv.append(jnp.concatenate(cols, axis=1))
    y = [((ui * svi) * _silu(gi)).astype(BF16) for ui, svi, gi in zip(u, sv, gate)]
    for rs, h, yi in zip(subs, h1, y):
        h2 = h + jnp.dot(yi, woo_ref[...], preferred_element_type=F32)
        o_ref[rs, :] = _rmsnorm(h2, fg_ref[...])


def _layer1(x2d, ya, yb, woe, g1, wio, lng, lnb, sgw, sgb_full, woo, fg, tm):
    t = x2d.shape[0]
    row = lambda i: (i, 0)
    const = lambda i: (0, 0)
    vec = pl.BlockSpec((1, D_MODEL), const)
    return pl.pallas_call(
        _layer1_kernel,
        grid=(t // tm,),
        in_specs=[pl.BlockSpec((tm, D_MODEL), row),
                  pl.BlockSpec((tm, RW_WIDTH), row),
                  pl.BlockSpec((tm, AT_WIDTH), row),
                  pl.BlockSpec((D_MODEL, D_MODEL), const),
                  vec,
                  pl.BlockSpec((D_MODEL, 3 * SG_WIDTH), const),
                  vec, vec,
                  pl.BlockSpec((SG_GROUPS, SG_CHUNK, SG_CHUNK), lambda i: (0, 0, 0)),
                  pl.BlockSpec((SG_CHUNK, SG_WIDTH), const),
                  pl.BlockSpec((SG_WIDTH, D_MODEL), const),
                  vec],
        out_specs=pl.BlockSpec((tm, D_MODEL), row),
        out_shape=jax.ShapeDtypeStruct((t, D_MODEL), F32),
        compiler_params=pltpu.CompilerParams(dimension_semantics=("parallel",),
                                             vmem_limit_bytes=VMEM_LIMIT),
        name="gmlp_layer",
    )(x2d, ya, yb, woe, g1, wio, lng, lnb, sgw, sgb_full, woo, fg)


def _bias_tables(att_bias):
    shift = np.array([c * CHUNK for c in range(N_BIAS - 1)] + [(LEFT_CHUNKS + 1) * CHUNK])
    n = np.arange(-(CHUNK - 1), KEY_WIN)
    idx = np.clip(shift[:, None] - n[None, :], -REL_CLIP, REL_CLIP) + REL_CLIP
    diag = jnp.transpose(att_bias[:, idx], (1, 0, 2)).astype(F32) * LOG2E
    diag = jnp.pad(diag, ((0, 0), (0, 0), (0, DIAG_PAD - diag.shape[-1])))
    n_heads = att_bias.shape[0]
    return pl.pallas_call(
        _bias_unfold_kernel,
        grid=(N_BIAS,),
        in_specs=[pl.BlockSpec((None, n_heads, DIAG_PAD), lambda c: (c, 0, 0))],
        out_specs=pl.BlockSpec((None, n_heads * CHUNK, KEY_WIN), lambda c: (c, 0, 0)),
        out_shape=jax.ShapeDtypeStruct((N_BIAS, n_heads * CHUNK, KEY_WIN), F32),
        compiler_params=pltpu.CompilerParams(dimension_semantics=("parallel",)),
        name="bias_unfold",
    )(diag)


DIAG_PAD = 768


def _bias_unfold_kernel(diag_ref, o_ref):
    c = pl.program_id(0)
    kj = lax.broadcasted_iota(jnp.int32, (CHUNK, KEY_WIN), 1)
    late = c == N_BIAS - 1
    lo = jnp.where(late, CHUNK, 0)
    hi = jnp.where(late, KEY_WIN, (c + 1) * CHUNK)
    valid = jnp.logical_and(kj >= lo, kj < hi)
    for h in range(diag_ref.shape[0]):
        rows = jnp.broadcast_to(diag_ref[h:h + 1, :], (CHUNK, DIAG_PAD))
        unfolded = pltpu.roll(rows, DIAG_PAD - (CHUNK - 1), 1, stride=1, stride_axis=0)
        o_ref[h * CHUNK:(h + 1) * CHUNK, :] = jnp.where(valid, unfolded[:, :KEY_WIN], NEG_INF)


def kernel(x, norm_g, w_in_e, shift_mu, rw_w0, rw_w2, rw_a0, rw_a2, rw_kk, rw_ka, rw_rk, rw_lnx_g,
           rw_lnx_b, att_bias, w_out_e, w_in_o, sg_ln_g, sg_ln_b, sg_w, sg_b, w_out_o, final_g):
    b, s, d = x.shape
    assert d == D_MODEL and s % 512 == 0 and s >= KEY_WIN and b % RWKV_SEQS_PER_STEP == 0
    x2d = x.reshape(b * s, d)
    row = lambda a: a.reshape(1, -1).astype(F32)

    rw, ga, q, k, v, gb = _in_proj_even(x2d, row(norm_g[0]), w_in_e[0].astype(BF16), row(shift_mu[0]),
                                        seq_len=s, tm=512)
    r3 = lambda a: a.reshape(b, s, a.shape[-1])
    ya = _rwkv_pipelined(r3(rw), r3(ga), row(rw_w0[0]), rw_w2[0], row(rw_a0[0]), rw_a2[0], row(rw_kk[0]),
                         row(rw_ka[0]), row(rw_rk[0]), row(rw_lnx_g[0]), row(rw_lnx_b[0]),
                         nb=RWKV_SEQS_PER_STEP, rb=256)
    yb = _attn(r3(q), r3(k), r3(v), r3(gb), _bias_tables(att_bias[0]), cb=16)

    sgb_full = jnp.repeat(sg_b[0].T, SG_WIDTH // SG_GROUPS, axis=1).astype(F32)
    out = _layer1(x2d, ya.reshape(b * s, -1), yb.reshape(b * s, -1), w_out_e[0].astype(BF16),
                  row(norm_g[1]), w_in_o[0].astype(BF16), row(sg_ln_g[0]), row(sg_ln_b[0]), sg_w[0],
                  sgb_full, w_out_o[0].astype(BF16), row(final_g), tm=512)
    return out.reshape(b, s, d)
```
